```python
import jax, jax.numpy as jnp
from jax import lax
import numpy as np

D_MODEL = 2048
BATCH = 2
SEQ = 16384
DEPTH = 1
DEC_BATCH = 8
DEC_SEQ = 64
PAST_LEN = 4096

CHUNK = 64
N_META = 16
HGRN_HEADS = 8
HGRN_DK = 128
HGRN_DV = 128
HGRN_WIDTH = HGRN_HEADS * HGRN_DK
HGRN_BLOCK = CHUNK // 4
CONV_WIDTH = D_MODEL // 2
CONV_K = 31
D_FF = 4 * D_MODEL
EPS = 1e-6
SPLITS = [HGRN_WIDTH, 2 * HGRN_WIDTH, 3 * HGRN_WIDTH, 4 * HGRN_WIDTH,
          4 * HGRN_WIDTH + CONV_WIDTH, 4 * HGRN_WIDTH + 2 * CONV_WIDTH,
          4 * HGRN_WIDTH + 2 * CONV_WIDTH + D_MODEL]
N_IN = 4 * HGRN_WIDTH + 2 * CONV_WIDTH + 2 * D_MODEL

kernel_name = "hgrn2_conformer_gated_stream_step"


def rmsnorm(x, g):
    xf = x.astype(jnp.float32)
    y = xf * lax.rsqrt(jnp.mean(xf * xf, axis=-1, keepdims=True) + EPS)
    return (y * g.astype(jnp.float32)).astype(x.dtype)


def layernorm(x, g, b):
    xf = x.astype(jnp.float32)
    mu = jnp.mean(xf, axis=-1, keepdims=True)
    var = jnp.mean(jnp.square(xf - mu), axis=-1, keepdims=True)
    y = (xf - mu) * lax.rsqrt(var + EPS)
    return (y * g.astype(jnp.float32) + b.astype(jnp.float32)).astype(x.dtype)


def hgrn2_chunkwise(q, k, v, log_f, s0):
    B, T, H, DK = q.shape
    DV = v.shape[-1]
    L = HGRN_BLOCK
    pad = (-T) % L
    if pad:
        pw = ((0, 0), (0, pad), (0, 0), (0, 0))
        q = jnp.pad(q, pw); k = jnp.pad(k, pw); v = jnp.pad(v, pw); log_f = jnp.pad(log_f, pw)
    N = (T + pad) // L

    def blocks(a):
        return a.reshape(B, N, L, H, a.shape[-1]).transpose(1, 0, 3, 2, 4)

    mask = jnp.tril(jnp.ones((L, L), dtype=bool))

    def step(S, xs):
        qb, kb, vb, lfb = xs
        b = jnp.cumsum(lfb, axis=2)
        r = b[:, :, L // 2 - 1:L // 2]
        g = b[:, :, L - 1:]
        scores = jnp.einsum('bhtd,bhsd->bhts', qb * jnp.exp(b - r), kb * jnp.exp(r - b))
        scores = jnp.where(mask, scores, 0.0)
        o = (jnp.einsum('bhts,bhsv->bhtv', scores, vb)
             + jnp.einsum('bhtd,bhdv->bhtv', qb * jnp.exp(b), S))
        S = (jnp.exp(g)[:, :, 0, :, None] * S
             + jnp.einsum('bhsd,bhsv->bhdv', kb * jnp.exp(g - b), vb))
        return S, o

    S, o = lax.scan(step, s0, (blocks(q), blocks(k), blocks(v), blocks(log_f)))
    o = o.transpose(1, 0, 3, 2, 4).reshape(B, N * L, H, DV)[:, :T]
    return o, S


def trunk_layer(x, s0, conv_buf, lb, norm1_g, w_in, hgrn_norm_g, w_proj_h, dw_kernel, dw_bias,
                conv_ln_g, conv_ln_b, w_proj_c, w_out, norm2_g, w_ff1, w_ff2):
    B, T, _ = x.shape
    f32 = jnp.float32
    h = rmsnorm(x, norm1_g)
    z = jnp.einsum('btd,dn->btn', h, w_in)
    q, f_logit, val, og, ca, cb, gh, gc = jnp.split(z, SPLITS, axis=-1)

    f = lb + (1.0 - lb) * jax.nn.sigmoid(f_logit.astype(f32)).reshape(B, T, HGRN_HEADS, HGRN_DK)
    log_f = jnp.log(f)
    k = 1.0 - f
    qh = jax.nn.silu(q.astype(f32)).reshape(B, T, HGRN_HEADS, HGRN_DK)
    vh = val.astype(f32).reshape(B, T, HGRN_HEADS, HGRN_DV)
    o, s_new = hgrn2_chunkwise(qh, k, vh, log_f, s0.astype(f32))
    o = rmsnorm(o, hgrn_norm_g) * jax.nn.silu(og.astype(f32)).reshape(B, T, HGRN_HEADS, HGRN_DV)
    o = o.reshape(B, T, HGRN_HEADS * HGRN_DV).astype(x.dtype)
    branch_h = jnp.einsum('btc,cd->btd', o, w_proj_h)

    u = ca * jax.nn.sigmoid(cb)
    uc = jnp.concatenate([conv_buf.astype(u.dtype), u], axis=1)
    dw = lax.conv_general_dilated(uc, dw_kernel[:, None, :].astype(uc.dtype), window_strides=(1,),
                                  padding='VALID', dimension_numbers=('NWC', 'WIO', 'NWC'),
                                  feature_group_count=CONV_WIDTH) + dw_bias
    new_buf = uc[:, uc.shape[1] - (CONV_K - 1):]
    c = jax.nn.silu(layernorm(dw, conv_ln_g, conv_ln_b))
    branch_c = jnp.einsum('btc,cd->btd', c, w_proj_c)

    merged = jax.nn.sigmoid(gh) * branch_h + jax.nn.sigmoid(gc) * branch_c
    x = x + jnp.einsum('btd,de->bte', merged, w_out)

    hf = jnp.einsum('btd,df->btf', rmsnorm(x, norm2_g), w_ff1)
    x = x + jnp.einsum('btf,fd->btd', jnp.square(jax.nn.relu(hf)), w_ff2)
    return x, s_new, new_buf


def setup_inputs(seed: int = 0) -> dict:
    key = jax.random.key(seed)
    ks = jax.random.split(key, 20)
    nrm = jax.random.normal
    f32 = jnp.float32
    return {
        "x_prompt": nrm(ks[0], (BATCH, SEQ, D_MODEL), f32),
        "x_sample": nrm(ks[1], (DEC_BATCH, DEC_SEQ, D_MODEL), f32),
        "state_hgrn": 0.5 * nrm(ks[2], (DEPTH, DEC_BATCH, HGRN_HEADS, HGRN_DK, HGRN_DV), f32),
        "state_conv": nrm(ks[3], (DEPTH, DEC_BATCH, CONV_K - 1, CONV_WIDTH), f32),
        "meta_tokens": nrm(ks[4], (N_META, D_MODEL), f32),
        "norm1_g": 1.0 + 0.02 * nrm(ks[5], (DEPTH, D_MODEL), f32),
        "w_in": nrm(ks[6], (DEPTH, D_MODEL, N_IN), f32) * D_MODEL ** -0.5,
        "lb_logits": 0.1 * nrm(ks[7], (DEPTH + 1, HGRN_WIDTH), f32),
        "hgrn_norm_g": 1.0 + 0.02 * nrm(ks[8], (DEPTH, HGRN_DV), f32),
        "w_proj_h": nrm(ks[9], (DEPTH, HGRN_HEADS * HGRN_DV, D_MODEL), f32) * (HGRN_HEADS * HGRN_DV) ** -0.5,
        "dw_kernel": nrm(ks[10], (DEPTH, CONV_K, CONV_WIDTH), f32) * CONV_K ** -0.5,
        "dw_bias": 0.02 * nrm(ks[11], (DEPTH, CONV_WIDTH), f32),
        "conv_ln_g": 1.0 + 0.02 * nrm(ks[12], (DEPTH, CONV_WIDTH), f32),
        "conv_ln_b": 0.02 * nrm(ks[13], (DEPTH, CONV_WIDTH), f32),
        "w_proj_c": nrm(ks[14], (DEPTH, CONV_WIDTH, D_MODEL), f32) * CONV_WIDTH ** -0.5,
        "w_out": nrm(ks[15], (DEPTH, D_MODEL, D_MODEL), f32) * D_MODEL ** -0.5,
        "norm2_g": 1.0 + 0.02 * nrm(ks[16], (DEPTH, D_MODEL), f32),
        "w_ff1": nrm(ks[17], (DEPTH, D_MODEL, D_FF), f32) * D_MODEL ** -0.5,
        "w_ff2": nrm(ks[18], (DEPTH, D_FF, D_MODEL), f32) * D_FF ** -0.5,
        "final_norm_g": 1.0 + 0.02 * nrm(ks[19], (D_MODEL,), f32),
    }


def reference(x_prompt, x_sample, state_hgrn, state_conv, meta_tokens, norm1_g, w_in, lb_logits,
              hgrn_norm_g, w_proj_h, dw_kernel, dw_bias, conv_ln_g, conv_ln_b, w_proj_c, w_out,
              norm2_g, w_ff1, w_ff2, final_norm_g):
    lb_all = jnp.cumsum(jax.nn.softmax(lb_logits.astype(jnp.float32), axis=0), axis=0)

    xp = jnp.concatenate([jnp.broadcast_to(meta_tokens[None].astype(x_prompt.dtype), (BATCH, N_META, D_MODEL)),
                          x_prompt], axis=1)
    xs = x_sample
    hp_list, cp_list, hs_list, cs_list = [], [], [], []
    for l in range(DEPTH):
        lb = lb_all[l].reshape(HGRN_HEADS, HGRN_DK)
        wts = (lb, norm1_g[l], w_in[l], hgrn_norm_g[l], w_proj_h[l], dw_kernel[l], dw_bias[l],
               conv_ln_g[l], conv_ln_b[l], w_proj_c[l], w_out[l], norm2_g[l], w_ff1[l], w_ff2[l])
        s0p = jnp.zeros((BATCH, HGRN_HEADS, HGRN_DK, HGRN_DV), jnp.float32)
        c0p = jnp.zeros((BATCH, CONV_K - 1, CONV_WIDTH), xp.dtype)
        xp, sp, cp = trunk_layer(xp, s0p, c0p, *wts)
        xs, ss, cs = trunk_layer(xs, state_hgrn[l], state_conv[l], *wts)
        hp_list.append(sp.astype(state_hgrn.dtype)); cp_list.append(cp.astype(state_conv.dtype))
        hs_list.append(ss.astype(state_hgrn.dtype)); cs_list.append(cs.astype(state_conv.dtype))

    y_prompt = rmsnorm(xp, final_norm_g)[:, N_META:]
    y_sample = rmsnorm(xs, final_norm_g)
    new_hgrn_prompt = jnp.stack(hp_list, axis=0)
    new_conv_prompt = jnp.stack(cp_list, axis=0)
    new_hgrn_sample = jnp.stack(hs_list, axis=0)
    new_conv_sample = jnp.stack(cs_list, axis=0)
    return (y_prompt, y_sample, new_hgrn_prompt, new_conv_prompt, new_hgrn_sample, new_conv_sample)
```

```python
import functools

import jax
import jax.numpy as jnp
from jax import lax
from jax.experimental import pallas as pl
from jax.experimental.pallas import tpu as pltpu

F32 = jnp.float32
BF16 = jnp.bfloat16

D_MODEL = 2048
N_META = 16
HEADS = 8
DK = 128
DV = 128
HW = HEADS * DK
CW = D_MODEL // 2
CONV_K = 31
HIST = 32
D_FF = 4 * D_MODEL
N_IN = 4 * HW + 2 * CW + 2 * D_MODEL
EPS = 1e-6
SUB = 16
META_CHUNK = 128

VMEM_LIMIT = 56 * 1024 * 1024


def _cparams(sem):
    return pltpu.CompilerParams(dimension_semantics=sem, vmem_limit_bytes=VMEM_LIMIT)


def _rms(x, g):
    return x * lax.rsqrt(jnp.mean(x * x, axis=-1, keepdims=True) + EPS) * g


def _in_proj_kernel(x_ref, g_ref, w_ref, z_ref, h_ref):
    @pl.when(pl.program_id(1) == 0)
    def _():
        h_ref[...] = _rms(x_ref[...], g_ref[...]).astype(BF16)

    z_ref[...] = jnp.dot(h_ref[...], w_ref[...], preferred_element_type=F32)


def _in_proj(x, g, w, tm, tn):
    T = x.shape[0]
    return pl.pallas_call(
        _in_proj_kernel,
        grid=(T // tm, N_IN // tn),
        in_specs=[
            pl.BlockSpec((tm, D_MODEL), lambda i, j: (i, 0)),
            pl.BlockSpec((1, D_MODEL), lambda i, j: (0, 0)),
            pl.BlockSpec((D_MODEL, tn), lambda i, j: (0, j)),
        ],
        out_specs=pl.BlockSpec((tm, tn), lambda i, j: (i, j)),
        out_shape=jax.ShapeDtypeStruct((T, N_IN), F32),
        scratch_shapes=[pltpu.VMEM((tm, D_MODEL), BF16)],
        compiler_params=_cparams(("parallel", "arbitrary")),
        name="in_proj",
    )(x, g, w)


def _hgrn_kernel(q_ref, f_ref, v_ref, og_ref, lb_ref, gn_ref, s0_ref, o_ref, s_ref, st_ref, *, C):
    c = pl.program_id(1)
    n = C // SUB

    @pl.when(c == 0)
    def _():
        for h in range(HEADS):
            st_ref[h] = s0_ref[0, h].T

    row = lax.broadcasted_iota(jnp.int32, (C, C), 0)
    col = lax.broadcasted_iota(jnp.int32, (C, C), 1)
    tril = col <= row
    tril_b = tril.astype(BF16)

    lb = lb_ref[...]
    f = lb + (1.0 - lb) * jax.nn.sigmoid(f_ref[...])
    lf = jnp.log(f)
    hi = lf.astype(BF16)
    r1 = lf - hi.astype(F32)
    mid = r1.astype(BF16)
    lo = (r1 - mid.astype(F32)).astype(BF16)
    b_all = (jnp.dot(tril_b, hi, preferred_element_type=F32)
             + jnp.dot(tril_b, mid, preferred_element_type=F32)
             + jnp.dot(tril_b, lo, preferred_element_type=F32))

    blk = lax.broadcasted_iota(jnp.int32, (C, DK), 0) // SUB
    gn = gn_ref[...]
    for h in range(HEADS):
        hs = slice(h * DK, (h + 1) * DK)
        b = b_all[:, hs]
        k = 1.0 - f[:, hs]
        q = jax.nn.silu(q_ref[:, hs])
        v = v_ref[:, hs]
        vb = v.astype(BF16)
        g = b[C - 1:C, :]

        a_parts, b_parts, r_rows = [], [], []
        for j in range(n):
            m = j * SUB + SUB // 2 - 1
            r_j = b[m:m + 1, :]
            r_rows.append(jnp.broadcast_to(r_j, (SUB, DK)))
            qa = q[j * SUB:, :] * jnp.exp(b[j * SUB:, :] - r_j)
            if j:
                qa = jnp.concatenate([jnp.zeros((j * SUB, DK), F32), qa], axis=0)
            a_parts.append(qa.astype(BF16))
        r_full = jnp.concatenate(r_rows, axis=0) if n > 1 else r_rows[0]
        kt = k * jnp.exp(r_full - b)
        for j in range(n):
            b_parts.append(jnp.where(blk == j, kt, 0.0).astype(BF16))
        a_cat = jnp.concatenate(a_parts, axis=1) if n > 1 else a_parts[0]
        b_cat = jnp.concatenate(b_parts, axis=1) if n > 1 else b_parts[0]
        scores = lax.dot_general(a_cat, b_cat, (((1,), (1,)), ((), ())), preferred_element_type=F32)
        scores = jnp.where(tril, scores, 0.0)

        st = st_ref[h]
        o = (jnp.dot(scores.astype(BF16), vb, preferred_element_type=F32)
             + lax.dot_general((q * jnp.exp(b)).astype(BF16), st.astype(BF16),
                               (((1,), (1,)), ((), ())), preferred_element_type=F32))
        kg = (k * jnp.exp(g - b)).astype(BF16)
        st_ref[h] = st * jnp.exp(g) + jnp.dot(v.T.astype(BF16), kg, preferred_element_type=F32)

        o = _rms(o, gn) * jax.nn.silu(og_ref[:, hs])
        o_ref[:, hs] = o.astype(o_ref.dtype)

    @pl.when(c == pl.num_programs(1) - 1)
    def _():
        for h in range(HEADS):
            s_ref[0, h] = st_ref[h].T


def _hgrn(z, lb, gn, s0, B, T, C, shared_state):
    nc = T // C
    s_idx = (lambda b, c: (0, 0, 0, 0)) if shared_state else (lambda b, c: (b, 0, 0, 0))

    def zcol(k):
        return pl.BlockSpec((C, HW), lambda b, c: (b * nc + c, k))

    return pl.pallas_call(
        functools.partial(_hgrn_kernel, C=C),
        grid=(B, nc),
        in_specs=[
            zcol(0), zcol(1), zcol(2), zcol(3),
            pl.BlockSpec((1, HW), lambda b, c: (0, 0)),
            pl.BlockSpec((1, DV), lambda b, c: (0, 0)),
            pl.BlockSpec((1, HEADS, DK, DV), s_idx),
        ],
        out_specs=[
            pl.BlockSpec((C, HW), lambda b, c: (b * nc + c, 0)),
            pl.BlockSpec((1, HEADS, DK, DV), lambda b, c: (b, 0, 0, 0)),
        ],
        out_shape=[
            jax.ShapeDtypeStruct((B * T, HW), BF16),
            jax.ShapeDtypeStruct((B, HEADS, DK, DV), F32),
        ],
        scratch_shapes=[pltpu.VMEM((HEADS, DV, DK), F32)],
        compiler_params=_cparams(("parallel", "arbitrary")),
        name="hgrn",
    )(z, z, z, z, lb, gn, s0)


def _conv_kernel(ca_ref, cb_ref, hist_ref, w_ref, bias_ref, lg_ref, lbias_ref, c_ref, nh_ref, uc_ref, *, tm):
    t = pl.program_id(1)

    @pl.when(t == 0)
    def _():
        uc_ref[0:HIST, :] = hist_ref[0]

    uc_ref[HIST:HIST + tm, :] = ca_ref[...] * jax.nn.sigmoid(cb_ref[...])

    off = HIST - (CONV_K - 1)
    acc = jnp.broadcast_to(bias_ref[...], (tm, CW))
    for j in range(CONV_K):
        acc = acc + uc_ref[off + j:off + j + tm, :] * w_ref[j:j + 1, :]

    mu = jnp.mean(acc, axis=-1, keepdims=True)
    d = acc - mu
    var = jnp.mean(d * d, axis=-1, keepdims=True)
    y = d * lax.rsqrt(var + EPS) * lg_ref[...] + lbias_ref[...]
    c_ref[...] = jax.nn.silu(y).astype(c_ref.dtype)

    tail = uc_ref[tm:tm + HIST, :]
    uc_ref[0:HIST, :] = tail

    @pl.when(t == pl.num_programs(1) - 1)
    def _():
        nh_ref[0] = tail


def _conv(z, hist, w, bias, lg, lbias, B, T, tm, shared_hist):
    nt = T // tm
    h_idx = (lambda b, t: (0, 0, 0)) if shared_hist else (lambda b, t: (b, 0, 0))
    vec = pl.BlockSpec((1, CW), lambda b, t: (0, 0))
    return pl.pallas_call(
        functools.partial(_conv_kernel, tm=tm),
        grid=(B, nt),
        in_specs=[
            pl.BlockSpec((tm, CW), lambda b, t: (b * nt + t, 4)),
            pl.BlockSpec((tm, CW), lambda b, t: (b * nt + t, 5)),
            pl.BlockSpec((1, HIST, CW), h_idx),
            pl.BlockSpec((HIST, CW), lambda b, t: (0, 0)),
            vec, vec, vec,
        ],
        out_specs=[
            pl.BlockSpec((tm, CW), lambda b, t: (b * nt + t, 0)),
            pl.BlockSpec((1, HIST, CW), lambda b, t: (b, 0, 0)),
        ],
        out_shape=[
            jax.ShapeDtypeStruct((B * T, CW), BF16),
            jax.ShapeDtypeStruct((B, HIST, CW), F32),
        ],
        scratch_shapes=[pltpu.VMEM((HIST + tm, CW), F32)],
        compiler_params=_cparams(("parallel", "arbitrary")),
        name="conv",
    )(z, z, hist, w, bias, lg, lbias)


def _merge_kernel(x_ref, o_ref, c_ref, gh_ref, gc_ref, wh_ref, wc_ref, wo_ref, y_ref):
    bh = jnp.dot(o_ref[...], wh_ref[...], preferred_element_type=F32)
    bc = jnp.dot(c_ref[...], wc_ref[...], preferred_element_type=F32)
    m = jax.nn.sigmoid(gh_ref[...]) * bh + jax.nn.sigmoid(gc_ref[...]) * bc
    y_ref[...] = x_ref[...] + jnp.dot(m.astype(BF16), wo_ref[...], preferred_element_type=F32)


def _merge(x, o, c, z, wh, wc, wo, tm):
    T = x.shape[0]
    const = lambda i: (0, 0)
    return pl.pallas_call(
        _merge_kernel,
        grid=(T // tm,),
        in_specs=[
            pl.BlockSpec((tm, D_MODEL), lambda i: (i, 0)),
            pl.BlockSpec((tm, HW), lambda i: (i, 0)),
            pl.BlockSpec((tm, CW), lambda i: (i, 0)),
            pl.BlockSpec((tm, D_MODEL), lambda i: (i, 3)),
            pl.BlockSpec((tm, D_MODEL), lambda i: (i, 4)),
            pl.BlockSpec((HW, D_MODEL), const, pipeline_mode=pl.Buffered(1)),
            pl.BlockSpec((CW, D_MODEL), const, pipeline_mode=pl.Buffered(1)),
            pl.BlockSpec((D_MODEL, D_MODEL), const, pipeline_mode=pl.Buffered(1)),
        ],
        out_specs=pl.BlockSpec((tm, D_MODEL), lambda i: (i, 0)),
        out_shape=jax.ShapeDtypeStruct((T, D_MODEL), F32),
        compiler_params=_cparams(("parallel",)),
        name="merge",
    )(x, o, c, z, z, wh, wc, wo)


def _ffn_kernel(x_ref, g2_ref, w1_ref, w2_ref, gf_ref, y_ref, h_ref, acc_ref):
    j = pl.program_id(1)

    @pl.when(j == 0)
    def _():
        h_ref[...] = _rms(x_ref[...], g2_ref[...]).astype(BF16)
        acc_ref[...] = x_ref[...]

    hf = jnp.dot(h_ref[...], w1_ref[...], preferred_element_type=F32)
    a = jnp.square(jnp.maximum(hf, 0.0)).astype(BF16)
    acc_ref[...] += jnp.dot(a, w2_ref[...], preferred_element_type=F32)

    @pl.when(j == pl.num_programs(1) - 1)
    def _():
        y_ref[...] = _rms(acc_ref[...], gf_ref[...])


def _ffn(x, g2, w1, w2, gf, tm, tf):
    T = x.shape[0]
    vec = pl.BlockSpec((1, D_MODEL), lambda i, j: (0, 0))
    return pl.pallas_call(
        _ffn_kernel,
        grid=(T // tm, D_FF // tf),
        in_specs=[
            pl.BlockSpec((tm, D_MODEL), lambda i, j: (i, 0)),
            vec,
            pl.BlockSpec((D_MODEL, tf), lambda i, j: (0, j)),
            pl.BlockSpec((tf, D_MODEL), lambda i, j: (j, 0)),
            vec,
        ],
        out_specs=pl.BlockSpec((tm, D_MODEL), lambda i, j: (i, 0)),
        out_shape=jax.ShapeDtypeStruct((T, D_MODEL), F32),
        scratch_shapes=[pltpu.VMEM((tm, D_MODEL), BF16), pltpu.VMEM((tm, D_MODEL), F32)],
        compiler_params=_cparams(("parallel", "arbitrary")),
        name="ffn",
    )(x, g2, w1, w2, gf)


def _pad_hist(buf):
    return jnp.pad(buf, ((0, 0), (HIST - (CONV_K - 1), 0), (0, 0)))


def kernel(x_prompt, x_sample, state_hgrn, state_conv, meta_tokens, norm1_g, w_in, lb_logits, hgrn_norm_g,
           w_proj_h, dw_kernel, dw_bias, conv_ln_g, conv_ln_b, w_proj_c, w_out, norm2_g, w_ff1, w_ff2,
           final_norm_g):
    BP, TP, _ = x_prompt.shape
    BS, TS, _ = x_sample.shape

    lb = jnp.cumsum(jax.nn.softmax(lb_logits.astype(F32), axis=0), axis=0)[0].reshape(1, HW)
    g1 = norm1_g[0].reshape(1, D_MODEL)
    g2 = norm2_g[0].reshape(1, D_MODEL)
    gf = final_norm_g.reshape(1, D_MODEL)
    gn = hgrn_norm_g[0].reshape(1, DV)
    w_in_b = w_in[0].astype(BF16)
    wh_b = w_proj_h[0].astype(BF16)
    wc_b = w_proj_c[0].astype(BF16)
    wo_b = w_out[0].astype(BF16)
    w1_b = w_ff1[0].astype(BF16)
    w2_b = w_ff2[0].astype(BF16)
    dw = jnp.pad(dw_kernel[0], ((0, HIST - CONV_K), (0, 0)))
    dbias = dw_bias[0].reshape(1, CW)
    lg = conv_ln_g[0].reshape(1, CW)
    lbias = conv_ln_b[0].reshape(1, CW)

    def front(x2d, B, T, s0, hist, tm_in, chunk, tm_conv, shared):
        z = _in_proj(x2d, g1, w_in_b, tm_in, 1024)
        o, s_new = _hgrn(z, lb, gn, s0, B, T, chunk, shared)
        c, h_new = _conv(z, hist, dw, dbias, lg, lbias, B, T, tm_conv, shared)
        return z, o, c, s_new, h_new

    def back(x2d, z, o, c, tm_merge, tm_ffn):
        x1 = _merge(x2d, o, c, z, wh_b, wc_b, wo_b, tm_merge)
        return _ffn(x1, g2, w1_b, w2_b, gf, tm_ffn, 512)

    zero_s = jnp.zeros((1, HEADS, DK, DV), F32)
    zero_h = jnp.zeros((1, HIST, CW), F32)
    meta_pad = jnp.pad(meta_tokens.astype(F32), ((META_CHUNK - N_META, 0), (0, 0)))
    _, _, _, s_meta, h_meta = front(meta_pad, 1, META_CHUNK, zero_s, zero_h,
                                    META_CHUNK, META_CHUNK, META_CHUNK, False)

    xp = x_prompt.reshape(BP * TP, D_MODEL)
    zp, op, cp, sp, hp = front(xp, BP, TP, s_meta, h_meta, min(1024, BP * TP), min(128, TP), min(256, TP), True)
    yp = back(xp, zp, op, cp, min(256, BP * TP), min(512, BP * TP))

    xs = x_sample.reshape(BS * TS, D_MODEL)
    zs, os_, cs, ss, hs = front(xs, BS, TS, state_hgrn[0], _pad_hist(state_conv[0]), BS * TS, TS, TS, False)
    ys = back(xs, zs, os_, cs, min(256, BS * TS), BS * TS)

    off = HIST - (CONV_K - 1)
    return (yp.reshape(BP, TP, D_MODEL), ys.reshape(BS, TS, D_MODEL),
            sp[None], hp[:, off:][None], ss[None], hs[:, off:][None])
```

```python
import functools

import jax
import jax.numpy as jnp
from jax import lax
from jax.experimental import pallas as pl
from jax.experimental.pallas import tpu as pltpu

F32 = jnp.float32
BF16 = jnp.bfloat16

D_MODEL = 2048
N_META = 16
HEADS = 8
DK = 128
DV = 128
HW = HEADS * DK
CW = D_MODEL // 2
CONV_K = 31
D_FF = 4 * D_MODEL
N_IN = 4 * HW + 2 * CW + 2 * D_MODEL
EPS = 1e-6

SUBLANES = 8
LANES = 128
HIST = 32
HIST_OFF = HIST - (CONV_K - 1)
SUB = 16
HEAD_LAG = 2
META_CHUNK = 128
CONV_ROWS = 64
TN_IN = 1024

VMEM_LIMIT = 56 * 1024 * 1024


def _cparams(sem):
    return pltpu.CompilerParams(dimension_semantics=sem, vmem_limit_bytes=VMEM_LIMIT)


def _rms(x, g):
    return x * lax.rsqrt(jnp.mean(x * x, axis=-1, keepdims=True) + EPS) * g


def _in_proj_kernel(x_ref, g_ref, lb_ref, w_ref, z_ref, h_ref):
    j = pl.program_id(1)

    @pl.when(j == 0)
    def _():
        h_ref[...] = _rms(x_ref[...], g_ref[...]).astype(BF16)

    def proj():
        return jnp.dot(h_ref[...], w_ref[...], preferred_element_type=F32)

    @pl.when((j == 0) | (j == 3))
    def _():
        z_ref[...] = jax.nn.silu(proj())

    @pl.when(j == 1)
    def _():
        lb = lb_ref[...]
        z_ref[...] = jnp.log(lb + (1.0 - lb) * jax.nn.sigmoid(proj()))

    @pl.when((j == 2) | (j == 4) | (j == 5))
    def _():
        z_ref[...] = proj()

    @pl.when(j >= 6)
    def _():
        z_ref[...] = jax.nn.sigmoid(proj())


def _in_proj(x, g, lb, w, tm):
    T = x.shape[0]
    return pl.pallas_call(
        _in_proj_kernel,
        grid=(T // tm, N_IN // TN_IN),
        in_specs=[
            pl.BlockSpec((tm, D_MODEL), lambda i, j: (i, 0)),
            pl.BlockSpec((1, D_MODEL), lambda i, j: (0, 0)),
            pl.BlockSpec((1, HW), lambda i, j: (0, 0)),
            pl.BlockSpec((D_MODEL, TN_IN), lambda i, j: (0, j)),
        ],
        out_specs=pl.BlockSpec((tm, TN_IN), lambda i, j: (i, j)),
        out_shape=jax.ShapeDtypeStruct((T, N_IN), F32),
        scratch_shapes=[pltpu.VMEM((tm, D_MODEL), BF16)],
        compiler_params=_cparams(("parallel", "arbitrary")),
        name="in_proj",
    )(x, g, lb, w)


def _hgrn_kernel(q_ref, lf_ref, v_ref, og_ref, gn_ref, s0_ref, o_ref, s_ref, st_ref, *, C, nsub):
    c = pl.program_id(1)
    n = C // SUB
    mid = SUB // 2 - 1

    @pl.when(c == 0)
    def _():
        for h in range(HEADS):
            st_ref[h] = s0_ref[0, h].T

    row = lax.broadcasted_iota(jnp.int32, (C, C), 0)
    col = lax.broadcasted_iota(jnp.int32, (C, C), 1)
    tril = col <= row
    srow = lax.broadcasted_iota(jnp.int32, (SUBLANES, C), 0)
    scol = lax.broadcasted_iota(jnp.int32, (SUBLANES, C), 1)
    sel = jnp.concatenate([tril, scol <= srow * SUB + mid], axis=0).astype(BF16)

    def cumulative(ci):
        lf = lf_ref[ci * C:(ci + 1) * C, :]
        hi = lf.astype(BF16)
        r1 = lf - hi.astype(F32)
        md = r1.astype(BF16)
        lo = (r1 - md.astype(F32)).astype(BF16)
        return (jnp.dot(sel, hi, preferred_element_type=F32)
                + jnp.dot(sel, md, preferred_element_type=F32)
                + jnp.dot(sel, lo, preferred_element_type=F32))

    b_alls = [cumulative(ci) for ci in range(nsub)]
    gn = gn_ref[...]
    zeros_blk = jnp.zeros((SUB, DK), BF16)

    def rows(a, i):
        return a[i * SUB:(i + 1) * SUB, :]

    def bcast(a, i):
        return jnp.broadcast_to(a[i:i + 1, :], (SUB, DK))

    def first_half(ci, h):
        rs = slice(ci * C, (ci + 1) * C)
        hs = slice(h * DK, (h + 1) * DK)
        b = b_alls[ci][:C, hs]
        rmat = b_alls[ci][C:, hs]
        g = b[C - 1:C, :]
        k = 1.0 - jnp.exp(lf_ref[rs, hs])
        q = q_ref[rs, hs]

        r_full = jnp.concatenate([bcast(rmat, i) for i in range(n)], axis=0) if n > 1 else bcast(rmat, 0)
        e = q * jnp.exp(b - r_full)
        kt = k * jnp.exp(r_full - b)
        to_state = jnp.exp(rmat)
        to_end = jnp.exp(g - rmat)

        a_cols, b_cols, qs_rows, kg_rows = [], [], [], []
        for j in range(n):
            fj = jnp.exp(rmat - rmat[j:j + 1, :])
            blocks = [zeros_blk] * j + [rows(e, j).astype(BF16)]
            blocks += [(rows(e, i) * bcast(fj, i)).astype(BF16) for i in range(j + 1, n)]
            a_cols.append(jnp.concatenate(blocks, axis=0) if n > 1 else blocks[0])
            kb = rows(kt, j).astype(BF16)
            blocks = [zeros_blk] * j + [kb] + [zeros_blk] * (n - 1 - j)
            b_cols.append(jnp.concatenate(blocks, axis=0) if n > 1 else blocks[0])
            qs_rows.append((rows(e, j) * bcast(to_state, j)).astype(BF16))
            kg_rows.append((rows(kt, j) * bcast(to_end, j)).astype(BF16))
        a_cat = jnp.concatenate(a_cols, axis=1) if n > 1 else a_cols[0]
        b_cat = jnp.concatenate(b_cols, axis=1) if n > 1 else b_cols[0]
        qs = jnp.concatenate(qs_rows, axis=0) if n > 1 else qs_rows[0]
        kg = jnp.concatenate(kg_rows, axis=0) if n > 1 else kg_rows[0]

        scores = lax.dot_general(a_cat, b_cat, (((1,), (1,)), ((), ())), preferred_element_type=F32)
        st = st_ref[h]
        inter = lax.dot_general(qs, st.astype(BF16), (((1,), (1,)), ((), ())), preferred_element_type=F32)
        st_ref[h] = st * jnp.exp(g) + jnp.dot(v_ref[rs, hs].T.astype(BF16), kg, preferred_element_type=F32)
        return scores, inter

    def second_half(ci, h, scores, inter):
        rs = slice(ci * C, (ci + 1) * C)
        hs = slice(h * DK, (h + 1) * DK)
        p = jnp.where(tril, scores, 0.0).astype(BF16)
        o = jnp.dot(p, v_ref[rs, hs].astype(BF16), preferred_element_type=F32) + inter
        o_ref[rs, hs] = (_rms(o, gn) * og_ref[rs, hs]).astype(o_ref.dtype)

    items = [(ci, h) for ci in range(nsub) for h in range(HEADS)]
    pending = {}
    for step in range(len(items) + HEAD_LAG):
        if step < len(items):
            pending[step] = first_half(*items[step])
        if step >= HEAD_LAG:
            second_half(*items[step - HEAD_LAG], *pending.pop(step - HEAD_LAG))

    @pl.when(c == pl.num_programs(1) - 1)
    def _():
        for h in range(HEADS):
            s_ref[0, h] = st_ref[h].T


def _hgrn(z, gn, s0, B, T, C, nsub, shared_state):
    rows_per_step = C * nsub
    nc = T // rows_per_step
    s_idx = (lambda b, c: (0, 0, 0, 0)) if shared_state else (lambda b, c: (b, 0, 0, 0))

    def zcol(k):
        return pl.BlockSpec((rows_per_step, HW), lambda b, c: (b * nc + c, k))

    return pl.pallas_call(
        functools.partial(_hgrn_kernel, C=C, nsub=nsub),
        grid=(B, nc),
        in_specs=[
            zcol(0), zcol(1), zcol(2), zcol(3),
            pl.BlockSpec((1, DV), lambda b, c: (0, 0)),
            pl.BlockSpec((1, HEADS, DK, DV), s_idx),
        ],
        out_specs=[
            pl.BlockSpec((rows_per_step, HW), lambda b, c: (b * nc + c, 0)),
            pl.BlockSpec((1, HEADS, DK, DV), lambda b, c: (b, 0, 0, 0)),
        ],
        out_shape=[
            jax.ShapeDtypeStruct((B * T, HW), BF16),
            jax.ShapeDtypeStruct((B, HEADS, DK, DV), F32),
        ],
        scratch_shapes=[pltpu.VMEM((HEADS, DV, DK), F32)],
        compiler_params=_cparams(("parallel", "arbitrary")),
        name="hgrn",
    )(z, z, z, z, gn, s0)


def _conv_kernel(ca_ref, cb_ref, hist_ref, w_ref, bias_ref, lg_ref, lbias_ref, c_ref, nh_ref,
                 uc_ref, us_ref, dw_ref, *, tm):
    t = pl.program_id(1)
    span = tm + HIST - SUBLANES

    @pl.when(t == 0)
    def _():
        uc_ref[0:HIST, :] = hist_ref[0]

    uc_ref[HIST:HIST + tm, :] = ca_ref[...] * jax.nn.sigmoid(cb_ref[...])
    for r in range(1, SUBLANES):
        us_ref[r - 1] = uc_ref[r:r + span, :]

    rg = min(CONV_ROWS, tm)
    groups = rg // SUBLANES

    def chunk(ci, carry):
        i0 = pl.multiple_of(ci * rg, rg)
        for l in range(CW // LANES):
            cols = slice(l * LANES, (l + 1) * LANES)
            accs = [bias_ref[:, cols]] * groups
            for j in range(CONV_K):
                a, r = divmod(HIST_OFF + j, SUBLANES)
                wv = w_ref[j, :, cols]
                for gi in range(groups):
                    start = i0 + (a + gi) * SUBLANES
                    if r == 0:
                        u = uc_ref[pl.ds(start, SUBLANES), cols]
                    else:
                        u = us_ref[r - 1, pl.ds(start, SUBLANES), cols]
                    accs[gi] = accs[gi] + u * wv
            for gi in range(groups):
                dw_ref[pl.ds(i0 + gi * SUBLANES, SUBLANES), cols] = accs[gi]
        return carry

    lax.fori_loop(0, tm // rg, chunk, 0)

    acc = dw_ref[...]
    mu = jnp.mean(acc, axis=-1, keepdims=True)
    d = acc - mu
    var = jnp.mean(d * d, axis=-1, keepdims=True)
    y = d * lax.rsqrt(var + EPS) * lg_ref[...] + lbias_ref[...]
    c_ref[...] = jax.nn.silu(y).astype(c_ref.dtype)

    tail = uc_ref[tm:tm + HIST, :]
    uc_ref[0:HIST, :] = tail

    @pl.when(t == pl.num_programs(1) - 1)
    def _():
        nh_ref[0] = tail


def _conv(z, hist, w, bias, lg, lbias, B, T, tm, shared_hist):
    nt = T // tm
    h_idx = (lambda b, t: (0, 0, 0)) if shared_hist else (lambda b, t: (b, 0, 0))
    vec = pl.BlockSpec((1, CW), lambda b, t: (0, 0))
    return pl.pallas_call(
        functools.partial(_conv_kernel, tm=tm),
        grid=(B, nt),
        in_specs=[
            pl.BlockSpec((tm, CW), lambda b, t: (b * nt + t, 4)),
            pl.BlockSpec((tm, CW), lambda b, t: (b * nt + t, 5)),
            pl.BlockSpec((1, HIST, CW), h_idx),
            pl.BlockSpec((CONV_K, SUBLANES, CW), lambda b, t: (0, 0, 0)),
            pl.BlockSpec((SUBLANES, CW), lambda b, t: (0, 0)),
            vec, vec,
        ],
        out_specs=[
            pl.BlockSpec((tm, CW), lambda b, t: (b * nt + t, 0)),
            pl.BlockSpec((1, HIST, CW), lambda b, t: (b, 0, 0)),
        ],
        out_shape=[
            jax.ShapeDtypeStruct((B * T, CW), BF16),
            jax.ShapeDtypeStruct((B, HIST, CW), F32),
        ],
        scratch_shapes=[
            pltpu.VMEM((HIST + tm, CW), F32),
            pltpu.VMEM((SUBLANES - 1, tm + HIST - SUBLANES, CW), F32),
            pltpu.VMEM((tm, CW), F32),
        ],
        compiler_params=_cparams(("parallel", "arbitrary")),
        name="conv",
    )(z, z, hist, w, bias, lg, lbias)


def _merge_kernel(x_ref, o_ref, c_ref, gh_ref, gc_ref, wh_ref, wc_ref, wo_ref, y_ref):
    bh = jnp.dot(o_ref[...], wh_ref[...], preferred_element_type=F32)
    bc = jnp.dot(c_ref[...], wc_ref[...], preferred_element_type=F32)
    m = gh_ref[...] * bh + gc_ref[...] * bc
    y_ref[...] = x_ref[...] + jnp.dot(m.astype(BF16), wo_ref[...], preferred_element_type=F32)


def _merge(x, o, c, z, wh, wc, wo, tm):
    T = x.shape[0]
    const = lambda i: (0, 0)
    return pl.pallas_call(
        _merge_kernel,
        grid=(T // tm,),
        in_specs=[
            pl.BlockSpec((tm, D_MODEL), lambda i: (i, 0)),
            pl.BlockSpec((tm, HW), lambda i: (i, 0)),
            pl.BlockSpec((tm, CW), lambda i: (i, 0)),
            pl.BlockSpec((tm, D_MODEL), lambda i: (i, 3)),
            pl.BlockSpec((tm, D_MODEL), lambda i: (i, 4)),
            pl.BlockSpec((HW, D_MODEL), const, pipeline_mode=pl.Buffered(1)),
            pl.BlockSpec((CW, D_MODEL), const, pipeline_mode=pl.Buffered(1)),
            pl.BlockSpec((D_MODEL, D_MODEL), const, pipeline_mode=pl.Buffered(1)),
        ],
        out_specs=pl.BlockSpec((tm, D_MODEL), lambda i: (i, 0)),
        out_shape=jax.ShapeDtypeStruct((T, D_MODEL), F32),
        compiler_params=_cparams(("parallel",)),
        name="merge",
    )(x, o, c, z, z, wh, wc, wo)


def _ffn_kernel(x_ref, g2_ref, w1_ref, w2_ref, gf_ref, y_ref, h_ref, acc_ref):
    j = pl.program_id(1)

    @pl.when(j == 0)
    def _():
        h_ref[...] = _rms(x_ref[...], g2_ref[...]).astype(BF16)
        acc_ref[...] = x_ref[...]

    hf = jnp.dot(h_ref[...], w1_ref[...], preferred_element_type=F32)
    a = jnp.square(jnp.maximum(hf, 0.0)).astype(BF16)
    acc_ref[...] += jnp.dot(a, w2_ref[...], preferred_element_type=F32)

    @pl.when(j == pl.num_programs(1) - 1)
    def _():
        y_ref[...] = _rms(acc_ref[...], gf_ref[...])


def _ffn(x, g2, w1, w2, gf, tm, tf):
    T = x.shape[0]
    vec = pl.BlockSpec((1, D_MODEL), lambda i, j: (0, 0))
    return pl.pallas_call(
        _ffn_kernel,
        grid=(T // tm, D_FF // tf),
        in_specs=[
            pl.BlockSpec((tm, D_MODEL), lambda i, j: (i, 0)),
            vec,
            pl.BlockSpec((D_MODEL, tf), lambda i, j: (0, j)),
            pl.BlockSpec((tf, D_MODEL), lambda i, j: (j, 0)),
            vec,
        ],
        out_specs=pl.BlockSpec((tm, D_MODEL), lambda i, j: (i, 0)),
        out_shape=jax.ShapeDtypeStruct((T, D_MODEL), F32),
        scratch_shapes=[pltpu.VMEM((tm, D_MODEL), BF16), pltpu.VMEM((tm, D_MODEL), F32)],
        compiler_params=_cparams(("parallel", "arbitrary")),
        name="ffn",
    )(x, g2, w1, w2, gf)


def _tiles(rows, stream_len):
    return dict(
        in_proj=min(1024, rows),
        chunk=min(128, stream_len),
        chunks_per_step=2 if stream_len >= 256 else 1,
        conv=min(256, stream_len),
        merge=min(256, rows),
        ffn=min(512, rows),
        ffn_cols=1024,
    )


def kernel(x_prompt, x_sample, state_hgrn, state_conv, meta_tokens, norm1_g, w_in, lb_logits, hgrn_norm_g,
           w_proj_h, dw_kernel, dw_bias, conv_ln_g, conv_ln_b, w_proj_c, w_out, norm2_g, w_ff1, w_ff2,
           final_norm_g):
    BP, TP, _ = x_prompt.shape
    BS, TS, _ = x_sample.shape

    lb = jnp.cumsum(jax.nn.softmax(lb_logits.astype(F32), axis=0), axis=0)[0].reshape(1, HW)
    g1 = norm1_g[0].reshape(1, D_MODEL)
    g2 = norm2_g[0].reshape(1, D_MODEL)
    gf = final_norm_g.reshape(1, D_MODEL)
    gn = hgrn_norm_g[0].reshape(1, DV)
    w_in_b = w_in[0].astype(BF16)
    wh_b = w_proj_h[0].astype(BF16)
    wc_b = w_proj_c[0].astype(BF16)
    wo_b = w_out[0].astype(BF16)
    w1_b = w_ff1[0].astype(BF16)
    w2_b = w_ff2[0].astype(BF16)
    dw = jnp.broadcast_to(dw_kernel[0][:, None, :], (CONV_K, SUBLANES, CW))
    dbias = jnp.broadcast_to(dw_bias[0][None, :], (SUBLANES, CW))
    lg = conv_ln_g[0].reshape(1, CW)
    lbias = conv_ln_b[0].reshape(1, CW)

    def front(x2d, B, T, s0, hist, shared):
        tl = _tiles(B * T, T)
        z = _in_proj(x2d, g1, lb, w_in_b, tl["in_proj"])
        o, s_new = _hgrn(z, gn, s0, B, T, tl["chunk"], tl["chunks_per_step"], shared)
        c, h_new = _conv(z, hist, dw, dbias, lg, lbias, B, T, tl["conv"], shared)
        return z, o, c, s_new, h_new

    def back(x2d, z, o, c, B, T):
        tl = _tiles(B * T, T)
        x1 = _merge(x2d, o, c, z, wh_b, wc_b, wo_b, tl["merge"])
        return _ffn(x1, g2, w1_b, w2_b, gf, tl["ffn"], tl["ffn_cols"])

    zero_s = jnp.zeros((1, HEADS, DK, DV), F32)
    zero_h = jnp.zeros((1, HIST, CW), F32)
    meta_pad = jnp.pad(meta_tokens.astype(F32), ((META_CHUNK - N_META, 0), (0, 0)))
    _, _, _, s_meta, h_meta = front(meta_pad, 1, META_CHUNK, zero_s, zero_h, False)

    xp = x_prompt.reshape(BP * TP, D_MODEL)
    zp, op, cp, sp, hp = front(xp, BP, TP, s_meta, h_meta, True)
    yp = back(xp, zp, op, cp, BP, TP)

    xs = x_sample.reshape(BS * TS, D_MODEL)
    hist_s = jnp.pad(state_conv[0], ((0, 0), (HIST_OFF, 0), (0, 0)))
    zs, os_, cs, ss, hs = front(xs, BS, TS, state_hgrn[0], hist_s, False)
    ys = back(xs, zs, os_, cs, BS, TS)

    return (yp.reshape(BP, TP, D_MODEL), ys.reshape(BS, TS, D_MODEL),
            sp[None], hp[:, HIST_OFF:][None], ss[None], hs[:, HIST_OFF:][None])
```

```python
import functools

import jax
import jax.numpy as jnp
from jax import lax
from jax.experimental import pallas as pl
from jax.experimental.pallas import tpu as pltpu

F32 = jnp.float32
BF16 = jnp.bfloat16

D_MODEL = 2048
N_META = 16
HEADS = 8
DK = 128
DV = 128
HW = HEADS * DK
CW = D_MODEL // 2
CONV_K = 31
D_FF = 4 * D_MODEL
N_IN = 4 * HW + 2 * CW + 2 * D_MODEL
EPS = 1e-6

SUBLANES = 8
LANES = 128
HIST = 32
HIST_OFF = HIST - (CONV_K - 1)
SUB = 16
HEAD_LAG = 2
META_CHUNK = 128
CONV_ROWS = 64
TN_IN = 1024

VMEM_LIMIT = 56 * 1024 * 1024


def _cparams(sem):
    return pltpu.CompilerParams(dimension_semantics=sem, vmem_limit_bytes=VMEM_LIMIT)


def _rms(x, g):
    return x * lax.rsqrt(jnp.mean(x * x, axis=-1, keepdims=True) + EPS) * g


def _in_proj_kernel(x_ref, g_ref, lb_ref, w_ref, z_ref, h_ref):
    j = pl.program_id(1)

    @pl.when(j == 0)
    def _():
        h_ref[...] = _rms(x_ref[...], g_ref[...]).astype(BF16)

    def proj():
        return jnp.dot(h_ref[...], w_ref[...], preferred_element_type=F32)

    @pl.when((j == 0) | (j == 3))
    def _():
        z_ref[...] = jax.nn.silu(proj())

    @pl.when(j == 1)
    def _():
        lb = lb_ref[...]
        z_ref[...] = jnp.log(lb + (1.0 - lb) * jax.nn.sigmoid(proj()))

    @pl.when((j == 2) | (j >= 4))
    def _():
        z_ref[...] = proj()


def _in_proj(x, g, lb, w, tm):
    T = x.shape[0]
    return pl.pallas_call(
        _in_proj_kernel,
        grid=(T // tm, N_IN // TN_IN),
        in_specs=[
            pl.BlockSpec((tm, D_MODEL), lambda i, j: (i, 0)),
            pl.BlockSpec((1, D_MODEL), lambda i, j: (0, 0)),
            pl.BlockSpec((1, HW), lambda i, j: (0, 0)),
            pl.BlockSpec((D_MODEL, TN_IN), lambda i, j: (0, j)),
        ],
        out_specs=pl.BlockSpec((tm, TN_IN), lambda i, j: (i, j)),
        out_shape=jax.ShapeDtypeStruct((T, N_IN), F32),
        scratch_shapes=[pltpu.VMEM((tm, D_MODEL), BF16)],
        compiler_params=_cparams(("parallel", "arbitrary")),
        name="in_proj",
    )(x, g, lb, w)


def _hgrn_kernel(q_ref, lf_ref, v_ref, og_ref, gn_ref, s0_ref, o_ref, s_ref, st_ref, *, C, nsub):
    c = pl.program_id(1)
    n = C // SUB
    mid = SUB // 2 - 1

    @pl.when(c == 0)
    def _():
        for h in range(HEADS):
            st_ref[h] = s0_ref[0, h].T

    row = lax.broadcasted_iota(jnp.int32, (C, C), 0)
    col = lax.broadcasted_iota(jnp.int32, (C, C), 1)
    tril = col <= row
    srow = lax.broadcasted_iota(jnp.int32, (SUBLANES, C), 0)
    scol = lax.broadcasted_iota(jnp.int32, (SUBLANES, C), 1)
    sel = jnp.concatenate([tril, scol <= srow * SUB + mid], axis=0).astype(BF16)

    def cumulative(ci):
        lf = lf_ref[ci * C:(ci + 1) * C, :]
        hi = lf.astype(BF16)
        r1 = lf - hi.astype(F32)
        md = r1.astype(BF16)
        lo = (r1 - md.astype(F32)).astype(BF16)
        return (jnp.dot(sel, hi, preferred_element_type=F32)
                + jnp.dot(sel, md, preferred_element_type=F32)
                + jnp.dot(sel, lo, preferred_element_type=F32))

    b_alls = [cumulative(ci) for ci in range(nsub)]
    gn = gn_ref[...]
    zeros_blk = jnp.zeros((SUB, DK), BF16)

    def rows(a, i):
        return a[i * SUB:(i + 1) * SUB, :]

    def bcast(a, i):
        return jnp.broadcast_to(a[i:i + 1, :], (SUB, DK))

    def first_half(ci, h):
        rs = slice(ci * C, (ci + 1) * C)
        hs = slice(h * DK, (h + 1) * DK)
        b = b_alls[ci][:C, hs]
        rmat = b_alls[ci][C:, hs]
        g = b[C - 1:C, :]
        k = 1.0 - jnp.exp(lf_ref[rs, hs])
        q = q_ref[rs, hs]

        r_full = jnp.concatenate([bcast(rmat, i) for i in range(n)], axis=0) if n > 1 else bcast(rmat, 0)
        e = q * jnp.exp(b - r_full)
        kt = k * jnp.exp(r_full - b)
        to_state = jnp.exp(rmat)
        to_end = jnp.exp(g - rmat)

        a_cols, b_cols, qs_rows, kg_rows = [], [], [], []
        for j in range(n):
            fj = jnp.exp(rmat - rmat[j:j + 1, :])
            blocks = [zeros_blk] * j + [rows(e, j).astype(BF16)]
            blocks += [(rows(e, i) * bcast(fj, i)).astype(BF16) for i in range(j + 1, n)]
            a_cols.append(jnp.concatenate(blocks, axis=0) if n > 1 else blocks[0])
            kb = rows(kt, j).astype(BF16)
            blocks = [zeros_blk] * j + [kb] + [zeros_blk] * (n - 1 - j)
            b_cols.append(jnp.concatenate(blocks, axis=0) if n > 1 else blocks[0])
            qs_rows.append((rows(e, j) * bcast(to_state, j)).astype(BF16))
            kg_rows.append((rows(kt, j) * bcast(to_end, j)).astype(BF16))
        a_cat = jnp.concatenate(a_cols, axis=1) if n > 1 else a_cols[0]
        b_cat = jnp.concatenate(b_cols, axis=1) if n > 1 else b_cols[0]
        qs = jnp.concatenate(qs_rows, axis=0) if n > 1 else qs_rows[0]
        kg = jnp.concatenate(kg_rows, axis=0) if n > 1 else kg_rows[0]

        scores = lax.dot_general(a_cat, b_cat, (((1,), (1,)), ((), ())), preferred_element_type=F32)
        st = st_ref[h]
        inter = lax.dot_general(qs, st.astype(BF16), (((1,), (1,)), ((), ())), preferred_element_type=F32)
        st_ref[h] = st * jnp.exp(g) + jnp.dot(v_ref[rs, hs].T.astype(BF16), kg, preferred_element_type=F32)
        return scores, inter

    def second_half(ci, h, scores, inter):
        rs = slice(ci * C, (ci + 1) * C)
        hs = slice(h * DK, (h + 1) * DK)
        p = jnp.where(tril, scores, 0.0).astype(BF16)
        o = jnp.dot(p, v_ref[rs, hs].astype(BF16), preferred_element_type=F32) + inter
        o_ref[rs, hs] = (_rms(o, gn) * og_ref[rs, hs]).astype(o_ref.dtype)

    items = [(ci, h) for ci in range(nsub) for h in range(HEADS)]
    pending = {}
    for step in range(len(items) + HEAD_LAG):
        if step < len(items):
            pending[step] = first_half(*items[step])
        if step >= HEAD_LAG:
            second_half(*items[step - HEAD_LAG], *pending.pop(step - HEAD_LAG))

    @pl.when(c == pl.num_programs(1) - 1)
    def _():
        for h in range(HEADS):
            s_ref[0, h] = st_ref[h].T


def _hgrn(z, gn, s0, B, T, C, nsub, shared_state):
    rows_per_step = C * nsub
    nc = T // rows_per_step
    s_idx = (lambda b, c: (0, 0, 0, 0)) if shared_state else (lambda b, c: (b, 0, 0, 0))

    def zcol(k):
        return pl.BlockSpec((rows_per_step, HW), lambda b, c: (b * nc + c, k))

    return pl.pallas_call(
        functools.partial(_hgrn_kernel, C=C, nsub=nsub),
        grid=(B, nc),
        in_specs=[
            zcol(0), zcol(1), zcol(2), zcol(3),
            pl.BlockSpec((1, DV), lambda b, c: (0, 0)),
            pl.BlockSpec((1, HEADS, DK, DV), s_idx),
        ],
        out_specs=[
            pl.BlockSpec((rows_per_step, HW), lambda b, c: (b * nc + c, 0)),
            pl.BlockSpec((1, HEADS, DK, DV), lambda b, c: (b, 0, 0, 0)),
        ],
        out_shape=[
            jax.ShapeDtypeStruct((B * T, HW), BF16),
            jax.ShapeDtypeStruct((B, HEADS, DK, DV), F32),
        ],
        scratch_shapes=[pltpu.VMEM((HEADS, DV, DK), F32)],
        compiler_params=_cparams(("parallel", "arbitrary")),
        name="hgrn",
    )(z, z, z, z, gn, s0)


def _conv_kernel(ca_ref, cb_ref, hist_ref, w_ref, bias_ref, lg_ref, lbias_ref, c_ref, nh_ref,
                 uc_ref, dw_ref, *, tm):
    t = pl.program_id(1)
    n_cols = CW // LANES
    lane_cols = [slice(l * LANES, (l + 1) * LANES) for l in range(n_cols)]

    @pl.when(t == 0)
    def _():
        for l, cols in enumerate(lane_cols):
            uc_ref[l, 0:HIST, :] = hist_ref[0, :, cols]

    u = ca_ref[...] * jax.nn.sigmoid(cb_ref[...])
    for l, cols in enumerate(lane_cols):
        uc_ref[l, HIST:HIST + tm, :] = u[:, cols]

    rows = min(CONV_ROWS, tm)
    groups = rows // SUBLANES

    def lane_column(l, carry):
        for batch in range(tm // rows):
            accs = [bias_ref[l]] * groups
            for j in range(CONV_K):
                wv = w_ref[l, j]
                for g in range(groups):
                    lo = batch * rows + g * SUBLANES + HIST_OFF + j
                    accs[g] = accs[g] + uc_ref[l, lo:lo + SUBLANES, :] * wv
            for g in range(groups):
                lo = batch * rows + g * SUBLANES
                dw_ref[l, lo:lo + SUBLANES, :] = accs[g]
        return carry

    lax.fori_loop(0, n_cols, lane_column, 0)

    total = dw_ref[0]
    for l in range(1, n_cols):
        total = total + dw_ref[l]
    mu = jnp.sum(total, axis=-1, keepdims=True) * (1.0 / CW)
    sq = jnp.square(dw_ref[0] - mu)
    for l in range(1, n_cols):
        sq = sq + jnp.square(dw_ref[l] - mu)
    inv = lax.rsqrt(jnp.sum(sq, axis=-1, keepdims=True) * (1.0 / CW) + EPS)
    for l, cols in enumerate(lane_cols):
        y = (dw_ref[l] - mu) * inv * lg_ref[:, cols] + lbias_ref[:, cols]
        c_ref[:, cols] = jax.nn.silu(y).astype(c_ref.dtype)

    tails = [uc_ref[l, tm:tm + HIST, :] for l in range(n_cols)]
    for l in range(n_cols):
        uc_ref[l, 0:HIST, :] = tails[l]

    @pl.when(t == pl.num_programs(1) - 1)
    def _():
        for l, cols in enumerate(lane_cols):
            nh_ref[0, :, cols] = tails[l]


def _conv(z, hist, w, bias, lg, lbias, B, T, tm, shared_hist):
    nt = T // tm
    n_cols = CW // LANES
    h_idx = (lambda b, t: (0, 0, 0)) if shared_hist else (lambda b, t: (b, 0, 0))
    vec = pl.BlockSpec((1, CW), lambda b, t: (0, 0))
    return pl.pallas_call(
        functools.partial(_conv_kernel, tm=tm),
        grid=(B, nt),
        in_specs=[
            pl.BlockSpec((tm, CW), lambda b, t: (b * nt + t, 4)),
            pl.BlockSpec((tm, CW), lambda b, t: (b * nt + t, 5)),
            pl.BlockSpec((1, HIST, CW), h_idx),
            pl.BlockSpec((n_cols, CONV_K, SUBLANES, LANES), lambda b, t: (0, 0, 0, 0)),
            pl.BlockSpec((n_cols, SUBLANES, LANES), lambda b, t: (0, 0, 0)),
            vec, vec,
        ],
        out_specs=[
            pl.BlockSpec((tm, CW), lambda b, t: (b * nt + t, 0)),
            pl.BlockSpec((1, HIST, CW), lambda b, t: (b, 0, 0)),
        ],
        out_shape=[
            jax.ShapeDtypeStruct((B * T, CW), BF16),
            jax.ShapeDtypeStruct((B, HIST, CW), F32),
        ],
        scratch_shapes=[
            pltpu.VMEM((n_cols, HIST + tm, LANES), F32),
            pltpu.VMEM((n_cols, tm, LANES), F32),
        ],
        compiler_params=_cparams(("parallel", "arbitrary")),
        name="conv",
    )(z, z, hist, w, bias, lg, lbias)


def _merge_kernel(x_ref, o_ref, c_ref, gh_ref, gc_ref, wh_ref, wc_ref, wo_ref, y_ref):
    bh = jnp.dot(o_ref[...], wh_ref[...], preferred_element_type=F32)
    bc = jnp.dot(c_ref[...], wc_ref[...], preferred_element_type=F32)
    m = jax.nn.sigmoid(gh_ref[...]) * bh + jax.nn.sigmoid(gc_ref[...]) * bc
    y_ref[...] = x_ref[...] + jnp.dot(m.astype(BF16), wo_ref[...], preferred_element_type=F32)


def _merge(x, o, c, z, wh, wc, wo, tm):
    T = x.shape[0]
    const = lambda i: (0, 0)
    return pl.pallas_call(
        _merge_kernel,
        grid=(T // tm,),
        in_specs=[
            pl.BlockSpec((tm, D_MODEL), lambda i: (i, 0)),
            pl.BlockSpec((tm, HW), lambda i: (i, 0)),
            pl.BlockSpec((tm, CW), lambda i: (i, 0)),
            pl.BlockSpec((tm, D_MODEL), lambda i: (i, 3)),
            pl.BlockSpec((tm, D_MODEL), lambda i: (i, 4)),
            pl.BlockSpec((HW, D_MODEL), const, pipeline_mode=pl.Buffered(1)),
            pl.BlockSpec((CW, D_MODEL), const, pipeline_mode=pl.Buffered(1)),
            pl.BlockSpec((D_MODEL, D_MODEL), const, pipeline_mode=pl.Buffered(1)),
        ],
        out_specs=pl.BlockSpec((tm, D_MODEL), lambda i: (i, 0)),
        out_shape=jax.ShapeDtypeStruct((T, D_MODEL), F32),
        compiler_params=_cparams(("parallel",)),
        name="merge",
    )(x, o, c, z, z, wh, wc, wo)


def _ffn_kernel(x_ref, g2_ref, w1_ref, w2_ref, gf_ref, y_ref, h_ref, acc_ref):
    j = pl.program_id(1)

    @pl.when(j == 0)
    def _():
        h_ref[...] = _rms(x_ref[...], g2_ref[...]).astype(BF16)
        acc_ref[...] = x_ref[...]

    hf = jnp.dot(h_ref[...], w1_ref[...], preferred_element_type=F32)
    a = jnp.square(jnp.maximum(hf, 0.0)).astype(BF16)
    acc_ref[...] += jnp.dot(a, w2_ref[...], preferred_element_type=F32)

    @pl.when(j == pl.num_programs(1) - 1)
    def _():
        y_ref[...] = _rms(acc_ref[...], gf_ref[...])


def _ffn(x, g2, w1, w2, gf, tm, tf):
    T = x.shape[0]
    vec = pl.BlockSpec((1, D_MODEL), lambda i, j: (0, 0))
    return pl.pallas_call(
        _ffn_kernel,
        grid=(T // tm, D_FF // tf),
        in_specs=[
            pl.BlockSpec((tm, D_MODEL), lambda i, j: (i, 0)),
            vec,
            pl.BlockSpec((D_MODEL, tf), lambda i, j: (0, j)),
            pl.BlockSpec((tf, D_MODEL), lambda i, j: (j, 0)),
            vec,
        ],
        out_specs=pl.BlockSpec((tm, D_MODEL), lambda i, j: (i, 0)),
        out_shape=jax.ShapeDtypeStruct((T, D_MODEL), F32),
        scratch_shapes=[pltpu.VMEM((tm, D_MODEL), BF16), pltpu.VMEM((tm, D_MODEL), F32)],
        compiler_params=_cparams(("parallel", "arbitrary")),
        name="ffn",
    )(x, g2, w1, w2, gf)


def _tiles(rows, stream_len):
    return dict(
        in_proj=min(1024, rows),
        chunk=min(128, stream_len),
        chunks_per_step=4 if stream_len >= 512 else 1,
        conv=min(256, stream_len),
        merge=min(256, rows),
        ffn=min(512, rows),
        ffn_cols=1024,
    )


def kernel(x_prompt, x_sample, state_hgrn, state_conv, meta_tokens, norm1_g, w_in, lb_logits, hgrn_norm_g,
           w_proj_h, dw_kernel, dw_bias, conv_ln_g, conv_ln_b, w_proj_c, w_out, norm2_g, w_ff1, w_ff2,
           final_norm_g):
    BP, TP, _ = x_prompt.shape
    BS, TS, _ = x_sample.shape

    lb = jnp.cumsum(jax.nn.softmax(lb_logits.astype(F32), axis=0), axis=0)[0].reshape(1, HW)
    g1 = norm1_g[0].reshape(1, D_MODEL)
    g2 = norm2_g[0].reshape(1, D_MODEL)
    gf = final_norm_g.reshape(1, D_MODEL)
    gn = hgrn_norm_g[0].reshape(1, DV)
    w_in_b = w_in[0].astype(BF16)
    wh_b = w_proj_h[0].astype(BF16)
    wc_b = w_proj_c[0].astype(BF16)
    wo_b = w_out[0].astype(BF16)
    w1_b = w_ff1[0].astype(BF16)
    w2_b = w_ff2[0].astype(BF16)
    n_cols = CW // LANES
    dw = jnp.broadcast_to(dw_kernel[0].reshape(CONV_K, n_cols, 1, LANES).transpose(1, 0, 2, 3),
                          (n_cols, CONV_K, SUBLANES, LANES))
    dbias = jnp.broadcast_to(dw_bias[0].reshape(n_cols, 1, LANES), (n_cols, SUBLANES, LANES))
    lg = conv_ln_g[0].reshape(1, CW)
    lbias = conv_ln_b[0].reshape(1, CW)

    def layer(x2d, B, T, s0, hist, shared):
        tl = _tiles(B * T, T)
        z = _in_proj(x2d, g1, lb, w_in_b, tl["in_proj"])
        o, s_new = _hgrn(z, gn, s0, B, T, tl["chunk"], tl["chunks_per_step"], shared)
        c, h_new = _conv(z, hist, dw, dbias, lg, lbias, B, T, tl["conv"], shared)
        x1 = _merge(x2d, o, c, z, wh_b, wc_b, wo_b, tl["merge"])
        return x1, s_new, h_new

    def ffn(x1, B, T):
        tl = _tiles(B * T, T)
        return _ffn(x1, g2, w1_b, w2_b, gf, tl["ffn"], tl["ffn_cols"])

    zero_s = jnp.zeros((1, HEADS, DK, DV), F32)
    zero_h = jnp.zeros((1, HIST, CW), F32)
    meta_pad = jnp.pad(meta_tokens.astype(F32), ((META_CHUNK - N_META, 0), (0, 0)))
    _, s_meta, h_meta = layer(meta_pad, 1, META_CHUNK, zero_s, zero_h, False)

    xp = x_prompt.reshape(BP * TP, D_MODEL)
    x1p, sp, hp = layer(xp, BP, TP, s_meta, h_meta, True)
    yp = ffn(x1p, BP, TP)

    xs = x_sample.reshape(BS * TS, D_MODEL)
    hist_s = jnp.pad(state_conv[0], ((0, 0), (HIST_OFF, 0), (0, 0)))
    x1s, ss, hs = layer(xs, BS, TS, state_hgrn[0], hist_s, False)
    ys = ffn(x1s, BS, TS)

    return (yp.reshape(BP, TP, D_MODEL), ys.reshape(BS, TS, D_MODEL),
            sp[None], hp[:, HIST_OFF:][None], ss[None], hs[:, HIST_OFF:][None])
```

```python
import functools

import jax
import jax.numpy as jnp
from jax import lax
from jax.experimental import pallas as pl
from jax.experimental.pallas import tpu as pltpu

F32 = jnp.float32
BF16 = jnp.bfloat16

D_MODEL = 2048
N_META = 16
HEADS = 8
DK = 128
DV = 128
HW = HEADS * DK
CW = D_MODEL // 2
CONV_K = 31
D_FF = 4 * D_MODEL
N_IN = 4 * HW + 2 * CW + 2 * D_MODEL
EPS = 1e-6

SUBLANES = 8
LANES = 128
MXU_COLS = 256
HIST = 32
HIST_OFF = HIST - (CONV_K - 1)
SUB = 16
HEAD_LAG = 2
META_CHUNK = 128
CONV_ROWS = 64
TN_IN = 2048

VMEM_LIMIT = 56 * 1024 * 1024
VMEM_LIMIT_IN_PROJ = 60 * 1024 * 1024


def _cparams(sem, vmem_limit=VMEM_LIMIT):
    return pltpu.CompilerParams(dimension_semantics=sem, vmem_limit_bytes=vmem_limit)


def _rms(x, g):
    return x * lax.rsqrt(jnp.mean(x * x, axis=-1, keepdims=True) + EPS) * g


def _in_proj_kernel(x_ref, g_ref, lb_ref, w_ref, z_ref, h_ref):
    j = pl.program_id(1)
    lo = slice(0, HW)
    hi = slice(HW, TN_IN)

    @pl.when(j == 0)
    def _():
        h_ref[...] = _rms(x_ref[...], g_ref[...]).astype(BF16)

    def proj(cols):
        return jnp.dot(h_ref[...], w_ref[:, cols], preferred_element_type=F32)

    @pl.when(j == 0)
    def _():
        z_ref[:, lo] = jax.nn.silu(proj(lo))
        lb = lb_ref[...]
        z_ref[:, hi] = jnp.log(lb + (1.0 - lb) * jax.nn.sigmoid(proj(hi)))

    @pl.when(j == 1)
    def _():
        z_ref[:, lo] = proj(lo)
        z_ref[:, hi] = jax.nn.silu(proj(hi))

    @pl.when(j == 2)
    def _():
        gate = proj(hi)
        z_ref[:, hi] = gate
        z_ref[:, lo] = proj(lo) * jax.nn.sigmoid(gate)

    @pl.when(j >= 3)
    def _():
        z_ref[...] = proj(slice(0, TN_IN))


def _in_proj(x, g, lb, w, tm):
    T = x.shape[0]
    return pl.pallas_call(
        _in_proj_kernel,
        grid=(T // tm, N_IN // TN_IN),
        in_specs=[
            pl.BlockSpec((tm, D_MODEL), lambda i, j: (i, 0)),
            pl.BlockSpec((1, D_MODEL), lambda i, j: (0, 0)),
            pl.BlockSpec((1, HW), lambda i, j: (0, 0)),
            pl.BlockSpec((D_MODEL, TN_IN), lambda i, j: (0, j)),
        ],
        out_specs=pl.BlockSpec((tm, TN_IN), lambda i, j: (i, j)),
        out_shape=jax.ShapeDtypeStruct((T, N_IN), F32),
        scratch_shapes=[pltpu.VMEM((tm, D_MODEL), BF16)],
        compiler_params=_cparams(("parallel", "arbitrary"), VMEM_LIMIT_IN_PROJ),
        name="in_proj",
    )(x, g, lb, w)


def _hgrn_kernel(q_ref, lf_ref, v_ref, og_ref, gn_ref, s0_ref, o_ref, s_ref, st_ref, *, C, nsub):
    c = pl.program_id(1)
    n = C // SUB
    mid = SUB // 2 - 1

    @pl.when(c == 0)
    def _():
        for h in range(HEADS):
            st_ref[h] = s0_ref[0, h].T

    row = lax.broadcasted_iota(jnp.int32, (C, C), 0)
    col = lax.broadcasted_iota(jnp.int32, (C, C), 1)
    tril = col <= row
    srow = lax.broadcasted_iota(jnp.int32, (SUBLANES, C), 0)
    scol = lax.broadcasted_iota(jnp.int32, (SUBLANES, C), 1)
    sel = jnp.concatenate([tril, scol <= srow * SUB + mid], axis=0).astype(BF16)

    def cumulative(ci):
        lf = lf_ref[ci * C:(ci + 1) * C, :]
        hi = lf.astype(BF16)
        r1 = lf - hi.astype(F32)
        md = r1.astype(BF16)
        lo = (r1 - md.astype(F32)).astype(BF16)
        return (jnp.dot(sel, hi, preferred_element_type=F32)
                + jnp.dot(sel, md, preferred_element_type=F32)
                + jnp.dot(sel, lo, preferred_element_type=F32))

    b_alls = [cumulative(ci) for ci in range(nsub)]
    gn = gn_ref[...]
    zeros_blk = jnp.zeros((SUB, DK), BF16)

    def rows(a, i):
        return a[i * SUB:(i + 1) * SUB, :]

    def bcast(a, i):
        return jnp.broadcast_to(a[i:i + 1, :], (SUB, DK))

    def first_half(ci, h):
        rs = slice(ci * C, (ci + 1) * C)
        hs = slice(h * DK, (h + 1) * DK)
        b = b_alls[ci][:C, hs]
        rmat = b_alls[ci][C:, hs]
        g = b[C - 1:C, :]
        k = 1.0 - jnp.exp(lf_ref[rs, hs])
        q = q_ref[rs, hs]

        r_full = jnp.concatenate([bcast(rmat, i) for i in range(n)], axis=0) if n > 1 else bcast(rmat, 0)
        e = q * jnp.exp(b - r_full)
        kt = k * jnp.exp(r_full - b)
        to_state = jnp.exp(rmat)
        to_end = jnp.exp(g - rmat)

        a_cols, b_cols, qs_rows, kg_rows = [], [], [], []
        for j in range(n):
            fj = jnp.exp(rmat - rmat[j:j + 1, :])
            blocks = [zeros_blk] * j + [rows(e, j).astype(BF16)]
            blocks += [(rows(e, i) * bcast(fj, i)).astype(BF16) for i in range(j + 1, n)]
            a_cols.append(jnp.concatenate(blocks, axis=0) if n > 1 else blocks[0])
            kb = rows(kt, j).astype(BF16)
            blocks = [zeros_blk] * j + [kb] + [zeros_blk] * (n - 1 - j)
            b_cols.append(jnp.concatenate(blocks, axis=0) if n > 1 else blocks[0])
            qs_rows.append((rows(e, j) * bcast(to_state, j)).astype(BF16))
            kg_rows.append((rows(kt, j) * bcast(to_end, j)).astype(BF16))
        a_cat = jnp.concatenate(a_cols, axis=1) if n > 1 else a_cols[0]
        b_cat = jnp.concatenate(b_cols, axis=1) if n > 1 else b_cols[0]
        qs = jnp.concatenate(qs_rows, axis=0) if n > 1 else qs_rows[0]
        kg = jnp.concatenate(kg_rows, axis=0) if n > 1 else kg_rows[0]

        scores = lax.dot_general(a_cat, b_cat, (((1,), (1,)), ((), ())), preferred_element_type=F32)
        st = st_ref[h]
        inter = lax.dot_general(qs, st.astype(BF16), (((1,), (1,)), ((), ())), preferred_element_type=F32)
        st_ref[h] = st * jnp.exp(g) + jnp.dot(v_ref[rs, hs].T.astype(BF16), kg, preferred_element_type=F32)
        return scores, inter

    def second_half(ci, h, scores, inter):
        rs = slice(ci * C, (ci + 1) * C)
        hs = slice(h * DK, (h + 1) * DK)
        p = jnp.where(tril, scores, 0.0).astype(BF16)
        o = jnp.dot(p, v_ref[rs, hs].astype(BF16), preferred_element_type=F32) + inter
        o_ref[rs, hs] = (_rms(o, gn) * og_ref[rs, hs]).astype(o_ref.dtype)

    items = [(ci, h) for ci in range(nsub) for h in range(HEADS)]
    pending = {}
    for step in range(len(items) + HEAD_LAG):
        if step < len(items):
            pending[step] = first_half(*items[step])
        if step >= HEAD_LAG:
            second_half(*items[step - HEAD_LAG], *pending.pop(step - HEAD_LAG))

    @pl.when(c == pl.num_programs(1) - 1)
    def _():
        for h in range(HEADS):
            s_ref[0, h] = st_ref[h].T


def _hgrn(z, gn, s0, B, T, C, nsub, shared_state):
    rows_per_step = C * nsub
    nc = T // rows_per_step
    s_idx = (lambda b, c: (0, 0, 0, 0)) if shared_state else (lambda b, c: (b, 0, 0, 0))

    def zcol(k):
        return pl.BlockSpec((rows_per_step, HW), lambda b, c: (b * nc + c, k))

    return pl.pallas_call(
        functools.partial(_hgrn_kernel, C=C, nsub=nsub),
        grid=(B, nc),
        in_specs=[
            zcol(0), zcol(1), zcol(2), zcol(3),
            pl.BlockSpec((1, DV), lambda b, c: (0, 0)),
            pl.BlockSpec((1, HEADS, DK, DV), s_idx),
        ],
        out_specs=[
            pl.BlockSpec((rows_per_step, HW), lambda b, c: (b * nc + c, 0)),
            pl.BlockSpec((1, HEADS, DK, DV), lambda b, c: (b, 0, 0, 0)),
        ],
        out_shape=[
            jax.ShapeDtypeStruct((B * T, HW), BF16),
            jax.ShapeDtypeStruct((B, HEADS, DK, DV), F32),
        ],
        scratch_shapes=[pltpu.VMEM((HEADS, DV, DK), F32)],
        compiler_params=_cparams(("parallel", "arbitrary")),
        name="hgrn",
    )(z, z, z, z, gn, s0)


def _mix_kernel(u_ref, hist_ref, w_ref, bias_ref, lg_ref, lbias_ref,
                x_ref, o_ref, gh_ref, gc_ref, wh_ref, wc_ref, wo_ref,
                y_ref, nh_ref, uc_ref, dw_ref, c_next_ref, c_cur_ref, m_ref, *, tm, nt, n_tiles):
    s = pl.program_id(0)
    t = s % nt
    n_cols = CW // LANES
    lane_cols = [slice(l * LANES, (l + 1) * LANES) for l in range(n_cols)]

    @pl.when(s == 0)
    def _():
        c_next_ref[...] = jnp.zeros((tm, CW), BF16)

    @pl.when(t == 0)
    def _():
        for l, cols in enumerate(lane_cols):
            uc_ref[l, 0:HIST, :] = hist_ref[0, :, cols]

    c_cur_ref[...] = c_next_ref[...]
    for l, cols in enumerate(lane_cols):
        uc_ref[l, HIST:HIST + tm, :] = u_ref[:, cols]

    rows = min(CONV_ROWS, tm)
    groups = rows // SUBLANES

    def taps(l, batch):
        accs = [bias_ref[l]] * groups
        for j in range(CONV_K):
            wv = w_ref[l, j]
            for g in range(groups):
                lo = batch * rows + g * SUBLANES + HIST_OFF + j
                accs[g] = accs[g] + uc_ref[l, lo:lo + SUBLANES, :] * wv
        for g in range(groups):
            lo = batch * rows + g * SUBLANES
            dw_ref[l, lo:lo + SUBLANES, :] = accs[g]

    pieces = [(l, batch) for l in range(n_cols) for batch in range(tm // rows)]
    n_groups = D_MODEL // MXU_COLS
    per_dot_group = -(-len(pieces) // (2 * n_groups))

    def some_taps():
        for _ in range(per_dot_group):
            if pieces:
                taps(*pieces.pop(0))

    for k in range(n_groups):
        cols = slice(k * MXU_COLS, (k + 1) * MXU_COLS)
        bh = jnp.dot(o_ref[...], wh_ref[:, cols], preferred_element_type=F32)
        bc = jnp.dot(c_cur_ref[...], wc_ref[:, cols], preferred_element_type=F32)
        m = jax.nn.sigmoid(gh_ref[:, cols]) * bh + jax.nn.sigmoid(gc_ref[:, cols]) * bc
        m_ref[:, cols] = m.astype(BF16)
        some_taps()
    for k in range(n_groups):
        cols = slice(k * MXU_COLS, (k + 1) * MXU_COLS)
        y_ref[:, cols] = x_ref[:, cols] + jnp.dot(m_ref[...], wo_ref[:, cols], preferred_element_type=F32)
        some_taps()

    total = dw_ref[0]
    for l in range(1, n_cols):
        total = total + dw_ref[l]
    mu = jnp.sum(total, axis=-1, keepdims=True) * (1.0 / CW)
    sq = jnp.square(dw_ref[0] - mu)
    for l in range(1, n_cols):
        sq = sq + jnp.square(dw_ref[l] - mu)
    inv = lax.rsqrt(jnp.sum(sq, axis=-1, keepdims=True) * (1.0 / CW) + EPS)
    for l, cols in enumerate(lane_cols):
        ln = (dw_ref[l] - mu) * inv * lg_ref[:, cols] + lbias_ref[:, cols]
        c_next_ref[:, cols] = jax.nn.silu(ln).astype(BF16)

    tails = [uc_ref[l, tm:tm + HIST, :] for l in range(n_cols)]
    for l in range(n_cols):
        uc_ref[l, 0:HIST, :] = tails[l]

    @pl.when((t == nt - 1) & (s < n_tiles))
    def _():
        for l, cols in enumerate(lane_cols):
            nh_ref[0, :, cols] = tails[l]


def _mix(x, o, z, hist, w, bias, lg, lbias, wh, wc, wo, B, T, tm, shared_hist):
    nt = T // tm
    n_tiles = B * nt
    n_cols = CW // LANES
    conv_tile = lambda s: jnp.minimum(s, n_tiles - 1)
    merge_tile = lambda s: jnp.maximum(s - 1, 0)
    h_idx = (lambda s: (0, 0, 0)) if shared_hist else (lambda s: (conv_tile(s) // nt, 0, 0))
    const = lambda s: (0, 0)
    vec = pl.BlockSpec((1, CW), const)
    resident = functools.partial(pl.BlockSpec, index_map=const, pipeline_mode=pl.Buffered(1))
    return pl.pallas_call(
        functools.partial(_mix_kernel, tm=tm, nt=nt, n_tiles=n_tiles),
        grid=(n_tiles + 1,),
        in_specs=[
            pl.BlockSpec((tm, CW), lambda s: (conv_tile(s), 4)),
            pl.BlockSpec((1, HIST, CW), h_idx),
            pl.BlockSpec((n_cols, CONV_K, SUBLANES, LANES), lambda s: (0, 0, 0, 0)),
            pl.BlockSpec((n_cols, SUBLANES, LANES), lambda s: (0, 0, 0)),
            vec, vec,
            pl.BlockSpec((tm, D_MODEL), lambda s: (merge_tile(s), 0)),
            pl.BlockSpec((tm, HW), lambda s: (merge_tile(s), 0)),
            pl.BlockSpec((tm, D_MODEL), lambda s: (merge_tile(s), 3)),
            pl.BlockSpec((tm, D_MODEL), lambda s: (merge_tile(s), 4)),
            resident((HW, D_MODEL)),
            resident((CW, D_MODEL)),
            resident((D_MODEL, D_MODEL)),
        ],
        out_specs=[
            pl.BlockSpec((tm, D_MODEL), lambda s: (merge_tile(s), 0)),
            pl.BlockSpec((1, HIST, CW), lambda s: (conv_tile(s) // nt, 0, 0)),
        ],
        out_shape=[
            jax.ShapeDtypeStruct((B * T, D_MODEL), F32),
            jax.ShapeDtypeStruct((B, HIST, CW), F32),
        ],
        scratch_shapes=[
            pltpu.VMEM((n_cols, HIST + tm, LANES), F32),
            pltpu.VMEM((n_cols, tm, LANES), F32),
            pltpu.VMEM((tm, CW), BF16),
            pltpu.VMEM((tm, CW), BF16),
            pltpu.VMEM((tm, D_MODEL), BF16),
        ],
        compiler_params=_cparams(("arbitrary",)),
        name="mix",
    )(z, hist, w, bias, lg, lbias, x, o, z, z, wh, wc, wo)


def _ffn_kernel(x_ref, g2_ref, w1_ref, w2_ref, gf_ref, y_ref, h_ref, acc_ref):
    j = pl.program_id(1)

    @pl.when(j == 0)
    def _():
        h_ref[...] = _rms(x_ref[...], g2_ref[...]).astype(BF16)
        acc_ref[...] = x_ref[...]

    hf = jnp.dot(h_ref[...], w1_ref[...], preferred_element_type=F32)
    a = jnp.square(jnp.maximum(hf, 0.0)).astype(BF16)
    acc_ref[...] += jnp.dot(a, w2_ref[...], preferred_element_type=F32)

    @pl.when(j == pl.num_programs(1) - 1)
    def _():
        y_ref[...] = _rms(acc_ref[...], gf_ref[...])


def _ffn(x, g2, w1, w2, gf, tm, tf):
    T = x.shape[0]
    vec = pl.BlockSpec((1, D_MODEL), lambda i, j: (0, 0))
    return pl.pallas_call(
        _ffn_kernel,
        grid=(T // tm, D_FF // tf),
        in_specs=[
            pl.BlockSpec((tm, D_MODEL), lambda i, j: (i, 0)),
            vec,
            pl.BlockSpec((D_MODEL, tf), lambda i, j: (0, j)),
            pl.BlockSpec((tf, D_MODEL), lambda i, j: (j, 0)),
            vec,
        ],
        out_specs=pl.BlockSpec((tm, D_MODEL), lambda i, j: (i, 0)),
        out_shape=jax.ShapeDtypeStruct((T, D_MODEL), F32),
        scratch_shapes=[pltpu.VMEM((tm, D_MODEL), BF16), pltpu.VMEM((tm, D_MODEL), F32)],
        compiler_params=_cparams(("parallel", "arbitrary")),
        name="ffn",
    )(x, g2, w1, w2, gf)


def _tiles(rows, stream_len):
    return dict(
        in_proj=min(1024, rows),
        chunk=min(128, stream_len),
        chunks_per_step=4 if stream_len >= 512 else 1,
        mix=min(256, stream_len),
        ffn=min(512, rows),
        ffn_cols=1024,
    )


def kernel(x_prompt, x_sample, state_hgrn, state_conv, meta_tokens, norm1_g, w_in, lb_logits, hgrn_norm_g,
           w_proj_h, dw_kernel, dw_bias, conv_ln_g, conv_ln_b, w_proj_c, w_out, norm2_g, w_ff1, w_ff2,
           final_norm_g):
    BP, TP, _ = x_prompt.shape
    BS, TS, _ = x_sample.shape

    lb = jnp.cumsum(jax.nn.softmax(lb_logits.astype(F32), axis=0), axis=0)[0].reshape(1, HW)
    g1 = norm1_g[0].reshape(1, D_MODEL)
    g2 = norm2_g[0].reshape(1, D_MODEL)
    gf = final_norm_g.reshape(1, D_MODEL)
    gn = hgrn_norm_g[0].reshape(1, DV)
    w_in_b = w_in[0].astype(BF16)
    wh_b = w_proj_h[0].astype(BF16)
    wc_b = w_proj_c[0].astype(BF16)
    wo_b = w_out[0].astype(BF16)
    w1_b = w_ff1[0].astype(BF16)
    w2_b = w_ff2[0].astype(BF16)
    n_cols = CW // LANES
    dw = jnp.broadcast_to(dw_kernel[0].reshape(CONV_K, n_cols, 1, LANES).transpose(1, 0, 2, 3),
                          (n_cols, CONV_K, SUBLANES, LANES))
    dbias = jnp.broadcast_to(dw_bias[0].reshape(n_cols, 1, LANES), (n_cols, SUBLANES, LANES))
    lg = conv_ln_g[0].reshape(1, CW)
    lbias = conv_ln_b[0].reshape(1, CW)

    def layer(x2d, B, T, s0, hist, shared):
        tl = _tiles(B * T, T)
        z = _in_proj(x2d, g1, lb, w_in_b, tl["in_proj"])
        o, s_new = _hgrn(z, gn, s0, B, T, tl["chunk"], tl["chunks_per_step"], shared)
        x1, h_new = _mix(x2d, o, z, hist, dw, dbias, lg, lbias, wh_b, wc_b, wo_b, B, T, tl["mix"], shared)
        return x1, s_new, h_new

    def ffn(x1, B, T):
        tl = _tiles(B * T, T)
        return _ffn(x1, g2, w1_b, w2_b, gf, tl["ffn"], tl["ffn_cols"])

    zero_s = jnp.zeros((1, HEADS, DK, DV), F32)
    zero_h = jnp.zeros((1, HIST, CW), F32)
    meta_pad = jnp.pad(meta_tokens.astype(F32), ((META_CHUNK - N_META, 0), (0, 0)))
    _, s_meta, h_meta = layer(meta_pad, 1, META_CHUNK, zero_s, zero_h, False)

    xp = x_prompt.reshape(BP * TP, D_MODEL)
    x1p, sp, hp = layer(xp, BP, TP, s_meta, h_meta, True)
    yp = ffn(x1p, BP, TP)

    xs = x_sample.reshape(BS * TS, D_MODEL)
    hist_s = jnp.pad(state_conv[0], ((0, 0), (HIST_OFF, 0), (0, 0)))
    x1s, ss, hs = layer(xs, BS, TS, state_hgrn[0], hist_s, False)
    ys = ffn(x1s, BS, TS)

    return (yp.reshape(BP, TP, D_MODEL), ys.reshape(BS, TS, D_MODEL),
            sp[None], hp[:, HIST_OFF:][None], ss[None], hs[:, HIST_OFF:][None])
```

```python
import functools

import jax
import jax.numpy as jnp
from jax import lax
from jax.experimental import pallas as pl
from jax.experimental.pallas import tpu as pltpu

F32 = jnp.float32
BF16 = jnp.bfloat16

D_MODEL = 2048
N_META = 16
HEADS = 8
DK = 128
DV = 128
HW = HEADS * DK
CW = D_MODEL // 2
CONV_K = 31
D_FF = 4 * D_MODEL
N_IN = 4 * HW + 2 * CW + 2 * D_MODEL
EPS = 1e-6

SUBLANES = 8
LANES = 128
HIST = 32
HIST_OFF = HIST - (CONV_K - 1)
SUB = 16
HEAD_LAG = 2
META_CHUNK = 128
CONV_ROWS = 64
TN_IN = 2048

VMEM_LIMIT = 56 * 1024 * 1024
VMEM_LIMIT_WIDE = 60 * 1024 * 1024


def _cparams(sem, vmem_limit=VMEM_LIMIT):
    return pltpu.CompilerParams(dimension_semantics=sem, vmem_limit_bytes=vmem_limit)


def _rms(x, g):
    return x * lax.rsqrt(jnp.mean(x * x, axis=-1, keepdims=True) + EPS) * g


def _in_proj_kernel(x_ref, g_ref, lb_ref, w_ref, z_ref, h_ref):
    j = pl.program_id(1)
    lo = slice(0, HW)
    hi = slice(HW, TN_IN)

    @pl.when(j == 0)
    def _():
        h_ref[...] = _rms(x_ref[...], g_ref[...]).astype(BF16)

    def proj(cols):
        return jnp.dot(h_ref[...], w_ref[:, cols], preferred_element_type=F32)

    @pl.when(j == 0)
    def _():
        z_ref[:, lo] = jax.nn.silu(proj(lo))
        lb = lb_ref[...]
        z_ref[:, hi] = jnp.log(lb + (1.0 - lb) * jax.nn.sigmoid(proj(hi)))

    @pl.when(j == 1)
    def _():
        z_ref[:, lo] = proj(lo)
        z_ref[:, hi] = jax.nn.silu(proj(hi))

    @pl.when(j == 2)
    def _():
        gate = proj(hi)
        z_ref[:, hi] = gate
        z_ref[:, lo] = proj(lo) * jax.nn.sigmoid(gate)

    @pl.when(j >= 3)
    def _():
        z_ref[...] = proj(slice(0, TN_IN))


def _in_proj(x, g, lb, w, tm):
    T = x.shape[0]
    return pl.pallas_call(
        _in_proj_kernel,
        grid=(T // tm, N_IN // TN_IN),
        in_specs=[
            pl.BlockSpec((tm, D_MODEL), lambda i, j: (i, 0)),
            pl.BlockSpec((1, D_MODEL), lambda i, j: (0, 0)),
            pl.BlockSpec((1, HW), lambda i, j: (0, 0)),
            pl.BlockSpec((D_MODEL, TN_IN), lambda i, j: (0, j)),
        ],
        out_specs=pl.BlockSpec((tm, TN_IN), lambda i, j: (i, j)),
        out_shape=jax.ShapeDtypeStruct((T, N_IN), F32),
        scratch_shapes=[pltpu.VMEM((tm, D_MODEL), BF16)],
        compiler_params=_cparams(("parallel", "arbitrary"), VMEM_LIMIT_WIDE),
        name="in_proj",
    )(x, g, lb, w)


def _hgrn_kernel(q_ref, lf_ref, v_ref, og_ref, gn_ref, s0_ref, o_ref, s_ref, st_ref, *, C, nsub):
    c = pl.program_id(1)
    n = C // SUB
    mid = SUB // 2 - 1

    @pl.when(c == 0)
    def _():
        for h in range(HEADS):
            st_ref[h] = s0_ref[0, h].T

    row = lax.broadcasted_iota(jnp.int32, (C, C), 0)
    col = lax.broadcasted_iota(jnp.int32, (C, C), 1)
    tril = col <= row
    srow = lax.broadcasted_iota(jnp.int32, (SUBLANES, C), 0)
    scol = lax.broadcasted_iota(jnp.int32, (SUBLANES, C), 1)
    sel = jnp.concatenate([tril, scol <= srow * SUB + mid], axis=0).astype(BF16)

    def cumulative(ci):
        lf = lf_ref[ci * C:(ci + 1) * C, :]
        hi = lf.astype(BF16)
        r1 = lf - hi.astype(F32)
        md = r1.astype(BF16)
        lo = (r1 - md.astype(F32)).astype(BF16)
        return (jnp.dot(sel, hi, preferred_element_type=F32)
                + jnp.dot(sel, md, preferred_element_type=F32)
                + jnp.dot(sel, lo, preferred_element_type=F32))

    b_alls = [cumulative(ci) for ci in range(nsub)]
    gn = gn_ref[...]
    zeros_blk = jnp.zeros((SUB, DK), BF16)

    def rows(a, i):
        return a[i * SUB:(i + 1) * SUB, :]

    def bcast(a, i):
        return jnp.broadcast_to(a[i:i + 1, :], (SUB, DK))

    def first_half(ci, h):
        rs = slice(ci * C, (ci + 1) * C)
        hs = slice(h * DK, (h + 1) * DK)
        b = b_alls[ci][:C, hs]
        rmat = b_alls[ci][C:, hs]
        g = b[C - 1:C, :]
        k = 1.0 - jnp.exp(lf_ref[rs, hs])
        q = q_ref[rs, hs]

        r_full = jnp.concatenate([bcast(rmat, i) for i in range(n)], axis=0) if n > 1 else bcast(rmat, 0)
        e = q * jnp.exp(b - r_full)
        kt = k * jnp.exp(r_full - b)
        to_state = jnp.exp(rmat)
        to_end = jnp.exp(g - rmat)

        a_cols, b_cols, qs_rows, kg_rows = [], [], [], []
        for j in range(n):
            fj = jnp.exp(rmat - rmat[j:j + 1, :])
            blocks = [zeros_blk] * j + [rows(e, j).astype(BF16)]
            blocks += [(rows(e, i) * bcast(fj, i)).astype(BF16) for i in range(j + 1, n)]
            a_cols.append(jnp.concatenate(blocks, axis=0) if n > 1 else blocks[0])
            kb = rows(kt, j).astype(BF16)
            blocks = [zeros_blk] * j + [kb] + [zeros_blk] * (n - 1 - j)
            b_cols.append(jnp.concatenate(blocks, axis=0) if n > 1 else blocks[0])
            qs_rows.append((rows(e, j) * bcast(to_state, j)).astype(BF16))
            kg_rows.append((rows(kt, j) * bcast(to_end, j)).astype(BF16))
        a_cat = jnp.concatenate(a_cols, axis=1) if n > 1 else a_cols[0]
        b_cat = jnp.concatenate(b_cols, axis=1) if n > 1 else b_cols[0]
        qs = jnp.concatenate(qs_rows, axis=0) if n > 1 else qs_rows[0]
        kg = jnp.concatenate(kg_rows, axis=0) if n > 1 else kg_rows[0]

        scores = lax.dot_general(a_cat, b_cat, (((1,), (1,)), ((), ())), preferred_element_type=F32)
        st = st_ref[h]
        inter = lax.dot_general(qs, st.astype(BF16), (((1,), (1,)), ((), ())), preferred_element_type=F32)
        st_ref[h] = st * jnp.exp(g) + jnp.dot(v_ref[rs, hs].T.astype(BF16), kg, preferred_element_type=F32)
        return scores, inter

    def second_half(ci, h, scores, inter):
        rs = slice(ci * C, (ci + 1) * C)
        hs = slice(h * DK, (h + 1) * DK)
        p = jnp.where(tril, scores, 0.0).astype(BF16)
        o = jnp.dot(p, v_ref[rs, hs].astype(BF16), preferred_element_type=F32) + inter
        o_ref[rs, hs] = (_rms(o, gn) * og_ref[rs, hs]).astype(o_ref.dtype)

    items = [(ci, h) for ci in range(nsub) for h in range(HEADS)]
    pending = {}
    for step in range(len(items) + HEAD_LAG):
        if step < len(items):
            pending[step] = first_half(*items[step])
        if step >= HEAD_LAG:
            second_half(*items[step - HEAD_LAG], *pending.pop(step - HEAD_LAG))

    @pl.when(c == pl.num_programs(1) - 1)
    def _():
        for h in range(HEADS):
            s_ref[0, h] = st_ref[h].T


def _hgrn(z, gn, s0, B, T, C, nsub, shared_state):
    rows_per_step = C * nsub
    nc = T // rows_per_step
    s_idx = (lambda b, c: (0, 0, 0, 0)) if shared_state else (lambda b, c: (b, 0, 0, 0))

    def zcol(k):
        return pl.BlockSpec((rows_per_step, HW), lambda b, c: (b * nc + c, k))

    return pl.pallas_call(
        functools.partial(_hgrn_kernel, C=C, nsub=nsub),
        grid=(B, nc),
        in_specs=[
            zcol(0), zcol(1), zcol(2), zcol(3),
            pl.BlockSpec((1, DV), lambda b, c: (0, 0)),
            pl.BlockSpec((1, HEADS, DK, DV), s_idx),
        ],
        out_specs=[
            pl.BlockSpec((rows_per_step, HW), lambda b, c: (b * nc + c, 0)),
            pl.BlockSpec((1, HEADS, DK, DV), lambda b, c: (b, 0, 0, 0)),
        ],
        out_shape=[
            jax.ShapeDtypeStruct((B * T, HW), BF16),
            jax.ShapeDtypeStruct((B, HEADS, DK, DV), F32),
        ],
        scratch_shapes=[pltpu.VMEM((HEADS, DV, DK), F32)],
        compiler_params=_cparams(("parallel", "arbitrary")),
        name="hgrn",
    )(z, z, z, z, gn, s0)


def _conv_kernel(u_ref, hist_ref, w_ref, bias_ref, lg_ref, lbias_ref, c_ref, nh_ref,
                 uc_ref, dw_ref, *, tm):
    t = pl.program_id(1)
    n_cols = CW // LANES
    lane_cols = [slice(l * LANES, (l + 1) * LANES) for l in range(n_cols)]

    @pl.when(t == 0)
    def _():
        for l, cols in enumerate(lane_cols):
            uc_ref[l, 0:HIST, :] = hist_ref[0, :, cols]

    for l, cols in enumerate(lane_cols):
        uc_ref[l, HIST:HIST + tm, :] = u_ref[:, cols]

    rows = min(CONV_ROWS, tm)
    groups = rows // SUBLANES

    def lane_column(l, carry):
        for batch in range(tm // rows):
            accs = [bias_ref[l]] * groups
            for j in range(CONV_K):
                wv = w_ref[l, j]
                for g in range(groups):
                    lo = batch * rows + g * SUBLANES + HIST_OFF + j
                    accs[g] = accs[g] + uc_ref[l, lo:lo + SUBLANES, :] * wv
            for g in range(groups):
                lo = batch * rows + g * SUBLANES
                dw_ref[l, lo:lo + SUBLANES, :] = accs[g]
        return carry

    lax.fori_loop(0, n_cols, lane_column, 0)

    total = dw_ref[0]
    for l in range(1, n_cols):
        total = total + dw_ref[l]
    mu = jnp.sum(total, axis=-1, keepdims=True) * (1.0 / CW)
    sq = jnp.square(dw_ref[0] - mu)
    for l in range(1, n_cols):
        sq = sq + jnp.square(dw_ref[l] - mu)
    inv = lax.rsqrt(jnp.sum(sq, axis=-1, keepdims=True) * (1.0 / CW) + EPS)
    for l, cols in enumerate(lane_cols):
        y = (dw_ref[l] - mu) * inv * lg_ref[:, cols] + lbias_ref[:, cols]
        c_ref[:, cols] = jax.nn.silu(y).astype(c_ref.dtype)

    tails = [uc_ref[l, tm:tm + HIST, :] for l in range(n_cols)]
    for l in range(n_cols):
        uc_ref[l, 0:HIST, :] = tails[l]

    @pl.when(t == pl.num_programs(1) - 1)
    def _():
        for l, cols in enumerate(lane_cols):
            nh_ref[0, :, cols] = tails[l]


def _conv(z, hist, w, bias, lg, lbias, B, T, tm, shared_hist):
    nt = T // tm
    n_cols = CW // LANES
    h_idx = (lambda b, t: (0, 0, 0)) if shared_hist else (lambda b, t: (b, 0, 0))
    vec = pl.BlockSpec((1, CW), lambda b, t: (0, 0))
    return pl.pallas_call(
        functools.partial(_conv_kernel, tm=tm),
        grid=(B, nt),
        in_specs=[
            pl.BlockSpec((tm, CW), lambda b, t: (b * nt + t, 4)),
            pl.BlockSpec((1, HIST, CW), h_idx),
            pl.BlockSpec((n_cols, CONV_K, SUBLANES, LANES), lambda b, t: (0, 0, 0, 0)),
            pl.BlockSpec((n_cols, SUBLANES, LANES), lambda b, t: (0, 0, 0)),
            vec, vec,
        ],
        out_specs=[
            pl.BlockSpec((tm, CW), lambda b, t: (b * nt + t, 0)),
            pl.BlockSpec((1, HIST, CW), lambda b, t: (b, 0, 0)),
        ],
        out_shape=[
            jax.ShapeDtypeStruct((B * T, CW), BF16),
            jax.ShapeDtypeStruct((B, HIST, CW), F32),
        ],
        scratch_shapes=[
            pltpu.VMEM((n_cols, HIST + tm, LANES), F32),
            pltpu.VMEM((n_cols, tm, LANES), F32),
        ],
        compiler_params=_cparams(("parallel", "arbitrary")),
        name="conv",
    )(z, hist, w, bias, lg, lbias)


def _merge_kernel(x_ref, o_ref, c_ref, gh_ref, gc_ref, wh_ref, wc_ref, wo_ref, y_ref):
    bh = jnp.dot(o_ref[...], wh_ref[...], preferred_element_type=F32)
    bc = jnp.dot(c_ref[...], wc_ref[...], preferred_element_type=F32)
    m = jax.nn.sigmoid(gh_ref[...]) * bh + jax.nn.sigmoid(gc_ref[...]) * bc
    y_ref[...] = x_ref[...] + jnp.dot(m.astype(BF16), wo_ref[...], preferred_element_type=F32)


def _merge(x, o, c, z, wh, wc, wo, tm):
    T = x.shape[0]
    const = lambda i: (0, 0)
    return pl.pallas_call(
        _merge_kernel,
        grid=(T // tm,),
        in_specs=[
            pl.BlockSpec((tm, D_MODEL), lambda i: (i, 0)),
            pl.BlockSpec((tm, HW), lambda i: (i, 0)),
            pl.BlockSpec((tm, CW), lambda i: (i, 0)),
            pl.BlockSpec((tm, D_MODEL), lambda i: (i, 3)),
            pl.BlockSpec((tm, D_MODEL), lambda i: (i, 4)),
            pl.BlockSpec((HW, D_MODEL), const, pipeline_mode=pl.Buffered(1)),
            pl.BlockSpec((CW, D_MODEL), const, pipeline_mode=pl.Buffered(1)),
            pl.BlockSpec((D_MODEL, D_MODEL), const, pipeline_mode=pl.Buffered(1)),
        ],
        out_specs=pl.BlockSpec((tm, D_MODEL), lambda i: (i, 0)),
        out_shape=jax.ShapeDtypeStruct((T, D_MODEL), F32),
        compiler_params=_cparams(("parallel",)),
        name="merge",
    )(x, o, c, z, z, wh, wc, wo)


def _ffn_kernel(x_ref, g2_ref, w1_ref, w2_ref, gf_ref, y_ref, h_ref):
    j = pl.program_id(1)

    @pl.when(j == 0)
    def _():
        h_ref[...] = _rms(x_ref[...], g2_ref[...]).astype(BF16)
        y_ref[...] = x_ref[...]

    hf = jnp.dot(h_ref[...], w1_ref[...], preferred_element_type=F32)
    a = jnp.square(jnp.maximum(hf, 0.0)).astype(BF16)
    y_ref[...] += jnp.dot(a, w2_ref[...], preferred_element_type=F32)

    @pl.when(j == pl.num_programs(1) - 1)
    def _():
        y_ref[...] = _rms(y_ref[...], gf_ref[...])


def _ffn(x, g2, w1, w2, gf, tm, tf):
    T = x.shape[0]
    vec = pl.BlockSpec((1, D_MODEL), lambda i, j: (0, 0))
    return pl.pallas_call(
        _ffn_kernel,
        grid=(T // tm, D_FF // tf),
        in_specs=[
            pl.BlockSpec((tm, D_MODEL), lambda i, j: (i, 0)),
            vec,
            pl.BlockSpec((D_MODEL, tf), lambda i, j: (0, j)),
            pl.BlockSpec((tf, D_MODEL), lambda i, j: (j, 0)),
            vec,
        ],
        out_specs=pl.BlockSpec((tm, D_MODEL), lambda i, j: (i, 0)),
        out_shape=jax.ShapeDtypeStruct((T, D_MODEL), F32),
        scratch_shapes=[pltpu.VMEM((tm, D_MODEL), BF16)],
        compiler_params=_cparams(("parallel", "arbitrary"), VMEM_LIMIT_WIDE),
        name="ffn",
    )(x, g2, w1, w2, gf)


def _tiles(rows, stream_len):
    return dict(
        in_proj=min(1024, rows),
        chunk=min(128, stream_len),
        chunks_per_step=4 if stream_len >= 512 else 1,
        conv=min(256, stream_len),
        merge=min(256, rows),
        ffn=min(512, rows),
        ffn_cols=2048,
    )


def kernel(x_prompt, x_sample, state_hgrn, state_conv, meta_tokens, norm1_g, w_in, lb_logits, hgrn_norm_g,
           w_proj_h, dw_kernel, dw_bias, conv_ln_g, conv_ln_b, w_proj_c, w_out, norm2_g, w_ff1, w_ff2,
           final_norm_g):
    BP, TP, _ = x_prompt.shape
    BS, TS, _ = x_sample.shape

    lb = jnp.cumsum(jax.nn.softmax(lb_logits.astype(F32), axis=0), axis=0)[0].reshape(1, HW)
    g1 = norm1_g[0].reshape(1, D_MODEL)
    g2 = norm2_g[0].reshape(1, D_MODEL)
    gf = final_norm_g.reshape(1, D_MODEL)
    gn = hgrn_norm_g[0].reshape(1, DV)
    w_in_b = w_in[0].astype(BF16)
    wh_b = w_proj_h[0].astype(BF16)
    wc_b = w_proj_c[0].astype(BF16)
    wo_b = w_out[0].astype(BF16)
    w1_b = w_ff1[0].astype(BF16)
    w2_b = w_ff2[0].astype(BF16)
    n_cols = CW // LANES
    dw = jnp.broadcast_to(dw_kernel[0].reshape(CONV_K, n_cols, 1, LANES).transpose(1, 0, 2, 3),
                          (n_cols, CONV_K, SUBLANES, LANES))
    dbias = jnp.broadcast_to(dw_bias[0].reshape(n_cols, 1, LANES), (n_cols, SUBLANES, LANES))
    lg = conv_ln_g[0].reshape(1, CW)
    lbias = conv_ln_b[0].reshape(1, CW)

    def layer(x2d, B, T, s0, hist, shared):
        tl = _tiles(B * T, T)
        z = _in_proj(x2d, g1, lb, w_in_b, tl["in_proj"])
        o, s_new = _hgrn(z, gn, s0, B, T, tl["chunk"], tl["chunks_per_step"], shared)
        c, h_new = _conv(z, hist, dw, dbias, lg, lbias, B, T, tl["conv"], shared)
        x1 = _merge(x2d, o, c, z, wh_b, wc_b, wo_b, tl["merge"])
        return x1, s_new, h_new

    def ffn(x1, B, T):
        tl = _tiles(B * T, T)
        return _ffn(x1, g2, w1_b, w2_b, gf, tl["ffn"], tl["ffn_cols"])

    zero_s = jnp.zeros((1, HEADS, DK, DV), F32)
    zero_h = jnp.zeros((1, HIST, CW), F32)
    meta_pad = jnp.pad(meta_tokens.astype(F32), ((META_CHUNK - N_META, 0), (0, 0)))
    _, s_meta, h_meta = layer(meta_pad, 1, META_CHUNK, zero_s, zero_h, False)

    xp = x_prompt.reshape(BP * TP, D_MODEL)
    x1p, sp, hp = layer(xp, BP, TP, s_meta, h_meta, True)
    yp = ffn(x1p, BP, TP)

    xs = x_sample.reshape(BS * TS, D_MODEL)
    hist_s = jnp.pad(state_conv[0], ((0, 0), (HIST_OFF, 0), (0, 0)))
    x1s, ss, hs = layer(xs, BS, TS, state_hgrn[0], hist_s, False)
    ys = ffn(x1s, BS, TS)

    return (yp.reshape(BP, TP, D_MODEL), ys.reshape(BS, TS, D_MODEL),
            sp[None], hp[:, HIST_OFF:][None], ss[None], hs[:, HIST_OFF:][None])
```

```python
import functools

import jax
import jax.numpy as jnp
from jax import lax
from jax.experimental import pallas as pl
from jax.experimental.pallas import tpu as pltpu

F32 = jnp.float32
BF16 = jnp.bfloat16

D_MODEL = 2048
N_META = 16
HEADS = 8
DK = 128
DV = 128
HW = HEADS * DK
CW = D_MODEL // 2
CONV_K = 31
D_FF = 4 * D_MODEL
N_IN = 4 * HW + 2 * CW + 2 * D_MODEL
EPS = 1e-6

SUBLANES = 8
LANES = 128
HIST = 32
HIST_OFF = HIST - (CONV_K - 1)
SUB = 16
HEAD_LAG = 2
META_CHUNK = 128
CONV_ROWS = 64
TN_IN = 2048

VMEM_LIMIT = 56 * 1024 * 1024
VMEM_LIMIT_WIDE = 60 * 1024 * 1024


def _cparams(sem, vmem_limit=VMEM_LIMIT):
    return pltpu.CompilerParams(dimension_semantics=sem, vmem_limit_bytes=vmem_limit)


def _rms(x, g):
    return x * lax.rsqrt(jnp.mean(x * x, axis=-1, keepdims=True) + EPS) * g


def _in_proj_kernel(x_ref, g_ref, lb_ref, w_ref, z_ref, h_ref):
    j = pl.program_id(1)
    lo = slice(0, HW)
    hi = slice(HW, TN_IN)

    @pl.when(j == 0)
    def _():
        h_ref[...] = _rms(x_ref[...], g_ref[...]).astype(BF16)

    def proj(cols):
        return jnp.dot(h_ref[...], w_ref[:, cols], preferred_element_type=F32)

    @pl.when(j == 0)
    def _():
        z_ref[:, lo] = jax.nn.silu(proj(lo))
        lb = lb_ref[...]
        z_ref[:, hi] = jnp.log(lb + (1.0 - lb) * jax.nn.sigmoid(proj(hi)))

    @pl.when(j == 1)
    def _():
        z_ref[:, lo] = proj(lo)
        z_ref[:, hi] = jax.nn.silu(proj(hi))

    @pl.when(j == 2)
    def _():
        gate = proj(hi)
        z_ref[:, hi] = gate
        z_ref[:, lo] = proj(lo) * jax.nn.sigmoid(gate)

    @pl.when(j >= 3)
    def _():
        z_ref[...] = proj(slice(0, TN_IN))


def _in_proj(x, g, lb, w, tm):
    T = x.shape[0]
    return pl.pallas_call(
        _in_proj_kernel,
        grid=(T // tm, N_IN // TN_IN),
        in_specs=[
            pl.BlockSpec((tm, D_MODEL), lambda i, j: (i, 0)),
            pl.BlockSpec((1, D_MODEL), lambda i, j: (0, 0)),
            pl.BlockSpec((1, HW), lambda i, j: (0, 0)),
            pl.BlockSpec((D_MODEL, TN_IN), lambda i, j: (0, j)),
        ],
        out_specs=pl.BlockSpec((tm, TN_IN), lambda i, j: (i, j)),
        out_shape=jax.ShapeDtypeStruct((T, N_IN), F32),
        scratch_shapes=[pltpu.VMEM((tm, D_MODEL), BF16)],
        compiler_params=_cparams(("parallel", "arbitrary"), VMEM_LIMIT_WIDE),
        name="in_proj",
    )(x, g, lb, w)


def _hgrn_kernel(q_ref, lf_ref, v_ref, og_ref, gn_ref, s0_ref, o_ref, s_ref, st_ref, *, C, nsub):
    c = pl.program_id(1)
    n = C // SUB
    mid = SUB // 2 - 1

    @pl.when(c == 0)
    def _():
        for h in range(HEADS):
            st_ref[h] = s0_ref[0, h].T

    row = lax.broadcasted_iota(jnp.int32, (C, C), 0)
    col = lax.broadcasted_iota(jnp.int32, (C, C), 1)
    tril = col <= row
    srow = lax.broadcasted_iota(jnp.int32, (SUBLANES, C), 0)
    scol = lax.broadcasted_iota(jnp.int32, (SUBLANES, C), 1)
    sel = jnp.concatenate([tril, scol <= srow * SUB + mid], axis=0).astype(BF16)

    def cumulative(ci):
        lf = lf_ref[ci * C:(ci + 1) * C, :]
        hi = lf.astype(BF16)
        r1 = lf - hi.astype(F32)
        md = r1.astype(BF16)
        lo = (r1 - md.astype(F32)).astype(BF16)
        return (jnp.dot(sel, hi, preferred_element_type=F32)
                + jnp.dot(sel, md, preferred_element_type=F32)
                + jnp.dot(sel, lo, preferred_element_type=F32))

    b_alls = [cumulative(ci) for ci in range(nsub)]
    gn = gn_ref[...]
    zeros_blk = jnp.zeros((SUB, DK), BF16)

    def rows(a, i):
        return a[i * SUB:(i + 1) * SUB, :]

    def bcast(a, i):
        return jnp.broadcast_to(a[i:i + 1, :], (SUB, DK))

    def first_half(ci, h):
        rs = slice(ci * C, (ci + 1) * C)
        hs = slice(h * DK, (h + 1) * DK)
        b = b_alls[ci][:C, hs]
        rmat = b_alls[ci][C:, hs]
        g = b[C - 1:C, :]
        k = 1.0 - jnp.exp(lf_ref[rs, hs])
        q = q_ref[rs, hs]

        r_full = jnp.concatenate([bcast(rmat, i) for i in range(n)], axis=0) if n > 1 else bcast(rmat, 0)
        e = q * jnp.exp(b - r_full)
        kt = k * jnp.exp(r_full - b)
        to_state = jnp.exp(rmat)
        to_end = jnp.exp(g - rmat)

        a_cols, b_cols, qs_rows, kg_rows = [], [], [], []
        for j in range(n):
            fj = jnp.exp(rmat - rmat[j:j + 1, :])
            blocks = [zeros_blk] * j + [rows(e, j).astype(BF16)]
            blocks += [(rows(e, i) * bcast(fj, i)).astype(BF16) for i in range(j + 1, n)]
            a_cols.append(jnp.concatenate(blocks, axis=0) if n > 1 else blocks[0])
            kb = rows(kt, j).astype(BF16)
            blocks = [zeros_blk] * j + [kb] + [zeros_blk] * (n - 1 - j)
            b_cols.append(jnp.concatenate(blocks, axis=0) if n > 1 else blocks[0])
            qs_rows.append((rows(e, j) * bcast(to_state, j)).astype(BF16))
            kg_rows.append((rows(kt, j) * bcast(to_end, j)).astype(BF16))
        a_cat = jnp.concatenate(a_cols, axis=1) if n > 1 else a_cols[0]
        b_cat = jnp.concatenate(b_cols, axis=1) if n > 1 else b_cols[0]
        qs = jnp.concatenate(qs_rows, axis=0) if n > 1 else qs_rows[0]
        kg = jnp.concatenate(kg_rows, axis=0) if n > 1 else kg_rows[0]

        scores = lax.dot_general(a_cat, b_cat, (((1,), (1,)), ((), ())), preferred_element_type=F32)
        st = st_ref[h]
        inter = lax.dot_general(qs, st.astype(BF16), (((1,), (1,)), ((), ())), preferred_element_type=F32)
        st_ref[h] = st * jnp.exp(g) + jnp.dot(v_ref[rs, hs].T.astype(BF16), kg, preferred_element_type=F32)
        return scores, inter

    def second_half(ci, h, scores, inter):
        rs = slice(ci * C, (ci + 1) * C)
        hs = slice(h * DK, (h + 1) * DK)
        p = jnp.where(tril, scores, 0.0).astype(BF16)
        o = jnp.dot(p, v_ref[rs, hs].astype(BF16), preferred_element_type=F32) + inter
        o_ref[rs, hs] = (_rms(o, gn) * og_ref[rs, hs]).astype(o_ref.dtype)

    items = [(ci, h) for ci in range(nsub) for h in range(HEADS)]
    pending = {}
    for step in range(len(items) + HEAD_LAG):
        if step < len(items):
            pending[step] = first_half(*items[step])
        if step >= HEAD_LAG:
            second_half(*items[step - HEAD_LAG], *pending.pop(step - HEAD_LAG))

    @pl.when(c == pl.num_programs(1) - 1)
    def _():
        for h in range(HEADS):
            s_ref[0, h] = st_ref[h].T


def _hgrn(z, gn, s0, B, T, C, nsub, shared_state):
    rows_per_step = C * nsub
    nc = T // rows_per_step
    s_idx = (lambda b, c: (0, 0, 0, 0)) if shared_state else (lambda b, c: (b, 0, 0, 0))

    def zcol(k):
        return pl.BlockSpec((rows_per_step, HW), lambda b, c: (b * nc + c, k))

    return pl.pallas_call(
        functools.partial(_hgrn_kernel, C=C, nsub=nsub),
        grid=(B, nc),
        in_specs=[
            zcol(0), zcol(1), zcol(2), zcol(3),
            pl.BlockSpec((1, DV), lambda b, c: (0, 0)),
            pl.BlockSpec((1, HEADS, DK, DV), s_idx),
        ],
        out_specs=[
            pl.BlockSpec((rows_per_step, HW), lambda b, c: (b * nc + c, 0)),
            pl.BlockSpec((1, HEADS, DK, DV), lambda b, c: (b, 0, 0, 0)),
        ],
        out_shape=[
            jax.ShapeDtypeStruct((B * T, HW), BF16),
            jax.ShapeDtypeStruct((B, HEADS, DK, DV), F32),
        ],
        scratch_shapes=[pltpu.VMEM((HEADS, DV, DK), F32)],
        compiler_params=_cparams(("parallel", "arbitrary")),
        name="hgrn",
    )(z, z, z, z, gn, s0)


def _conv_kernel(u_ref, hist_ref, w_ref, bias_ref, lg_ref, lbias_ref, c_ref, nh_ref,
                 uc_ref, dw_ref, *, tm):
    t = pl.program_id(1)
    n_cols = CW // LANES
    lane_cols = [slice(l * LANES, (l + 1) * LANES) for l in range(n_cols)]

    @pl.when(t == 0)
    def _():
        for l, cols in enumerate(lane_cols):
            uc_ref[l, 0:HIST, :] = hist_ref[0, :, cols]

    for l, cols in enumerate(lane_cols):
        uc_ref[l, HIST:HIST + tm, :] = u_ref[:, cols]

    rows = min(CONV_ROWS, tm)
    groups = rows // SUBLANES

    def lane_column(l, carry):
        for batch in range(tm // rows):
            accs = [bias_ref[l]] * groups
            for j in range(CONV_K):
                wv = w_ref[l, j]
                for g in range(groups):
                    lo = batch * rows + g * SUBLANES + HIST_OFF + j
                    accs[g] = accs[g] + uc_ref[l, lo:lo + SUBLANES, :] * wv
            for g in range(groups):
                lo = batch * rows + g * SUBLANES
                dw_ref[l, lo:lo + SUBLANES, :] = accs[g]
        return carry

    lax.fori_loop(0, n_cols, lane_column, 0)

    total = dw_ref[0]
    for l in range(1, n_cols):
        total = total + dw_ref[l]
    mu = jnp.sum(total, axis=-1, keepdims=True) * (1.0 / CW)
    sq = jnp.square(dw_ref[0] - mu)
    for l in range(1, n_cols):
        sq = sq + jnp.square(dw_ref[l] - mu)
    inv = lax.rsqrt(jnp.sum(sq, axis=-1, keepdims=True) * (1.0 / CW) + EPS)
    for l, cols in enumerate(lane_cols):
        y = (dw_ref[l] - mu) * inv * lg_ref[:, cols] + lbias_ref[:, cols]
        c_ref[:, cols] = jax.nn.silu(y).astype(c_ref.dtype)

    tails = [uc_ref[l, tm:tm + HIST, :] for l in range(n_cols)]
    for l in range(n_cols):
        uc_ref[l, 0:HIST, :] = tails[l]

    @pl.when(t == pl.num_programs(1) - 1)
    def _():
        for l, cols in enumerate(lane_cols):
            nh_ref[0, :, cols] = tails[l]


def _conv(z, hist, w, bias, lg, lbias, B, T, tm, shared_hist):
    nt = T // tm
    n_cols = CW // LANES
    h_idx = (lambda b, t: (0, 0, 0)) if shared_hist else (lambda b, t: (b, 0, 0))
    vec = pl.BlockSpec((1, CW), lambda b, t: (0, 0))
    return pl.pallas_call(
        functools.partial(_conv_kernel, tm=tm),
        grid=(B, nt),
        in_specs=[
            pl.BlockSpec((tm, CW), lambda b, t: (b * nt + t, 4)),
            pl.BlockSpec((1, HIST, CW), h_idx),
            pl.BlockSpec((n_cols, CONV_K, SUBLANES, LANES), lambda b, t: (0, 0, 0, 0)),
            pl.BlockSpec((n_cols, SUBLANES, LANES), lambda b, t: (0, 0, 0)),
            vec, vec,
        ],
        out_specs=[
            pl.BlockSpec((tm, CW), lambda b, t: (b * nt + t, 0)),
            pl.BlockSpec((1, HIST, CW), lambda b, t: (b, 0, 0)),
        ],
        out_shape=[
            jax.ShapeDtypeStruct((B * T, CW), BF16),
            jax.ShapeDtypeStruct((B, HIST, CW), F32),
        ],
        scratch_shapes=[
            pltpu.VMEM((n_cols, HIST + tm, LANES), F32),
            pltpu.VMEM((n_cols, tm, LANES), F32),
        ],
        compiler_params=_cparams(("parallel", "arbitrary")),
        name="conv",
    )(z, hist, w, bias, lg, lbias)


def _merge_kernel(x_ref, o_ref, c_ref, gh_ref, gc_ref, wh_ref, wc_ref, wo_ref, y_ref):
    bh = jnp.dot(o_ref[...], wh_ref[...], preferred_element_type=F32)
    bc = jnp.dot(c_ref[...], wc_ref[...], preferred_element_type=F32)
    m = jax.nn.sigmoid(gh_ref[...]) * bh + jax.nn.sigmoid(gc_ref[...]) * bc
    y_ref[...] = x_ref[...] + jnp.dot(m.astype(BF16), wo_ref[...], preferred_element_type=F32)


def _merge(x, o, c, z, wh, wc, wo, tm):
    T = x.shape[0]
    const = lambda i: (0, 0)
    return pl.pallas_call(
        _merge_kernel,
        grid=(T // tm,),
        in_specs=[
            pl.BlockSpec((tm, D_MODEL), lambda i: (i, 0)),
            pl.BlockSpec((tm, HW), lambda i: (i, 0)),
            pl.BlockSpec((tm, CW), lambda i: (i, 0)),
            pl.BlockSpec((tm, D_MODEL), lambda i: (i, 3)),
            pl.BlockSpec((tm, D_MODEL), lambda i: (i, 4)),
            pl.BlockSpec((HW, D_MODEL), const, pipeline_mode=pl.Buffered(1)),
            pl.BlockSpec((CW, D_MODEL), const, pipeline_mode=pl.Buffered(1)),
            pl.BlockSpec((D_MODEL, D_MODEL), const, pipeline_mode=pl.Buffered(1)),
        ],
        out_specs=pl.BlockSpec((tm, D_MODEL), lambda i: (i, 0)),
        out_shape=jax.ShapeDtypeStruct((T, D_MODEL), F32),
        compiler_params=_cparams(("parallel",)),
        name="merge",
    )(x, o, c, z, z, wh, wc, wo)


def _ffn_kernel(x_ref, g2_ref, w1_ref, w2_ref, gf_ref, y_ref, h_ref):
    j = pl.program_id(1)

    def mlp(h):
        hf = jnp.dot(h, w1_ref[...], preferred_element_type=F32)
        a = jnp.square(jnp.maximum(hf, 0.0)).astype(BF16)
        return jnp.dot(a, w2_ref[...], preferred_element_type=F32)

    @pl.when(j == 0)
    def _():
        x = x_ref[...]
        h = _rms(x, g2_ref[...]).astype(BF16)
        h_ref[...] = h
        y_ref[...] = x + mlp(h)

    last = pl.num_programs(1) - 1

    @pl.when((j > 0) & (j < last))
    def _():
        y_ref[...] += mlp(h_ref[...])

    @pl.when(j == last)
    def _():
        y_ref[...] = _rms(y_ref[...] + mlp(h_ref[...]), gf_ref[...])


def _ffn(x, g2, w1, w2, gf, tm, tf):
    T = x.shape[0]
    vec = pl.BlockSpec((1, D_MODEL), lambda i, j: (0, 0))
    return pl.pallas_call(
        _ffn_kernel,
        grid=(T // tm, D_FF // tf),
        in_specs=[
            pl.BlockSpec((tm, D_MODEL), lambda i, j: (i, 0)),
            vec,
            pl.BlockSpec((D_MODEL, tf), lambda i, j: (0, j)),
            pl.BlockSpec((tf, D_MODEL), lambda i, j: (j, 0)),
            vec,
        ],
        out_specs=pl.BlockSpec((tm, D_MODEL), lambda i, j: (i, 0)),
        out_shape=jax.ShapeDtypeStruct((T, D_MODEL), F32),
        scratch_shapes=[pltpu.VMEM((tm, D_MODEL), BF16)],
        compiler_params=_cparams(("parallel", "arbitrary"), VMEM_LIMIT_WIDE),
        name="ffn",
    )(x, g2, w1, w2, gf)


def _tiles(rows, stream_len):
    return dict(
        in_proj=min(1024, rows),
        chunk=min(128, stream_len),
        chunks_per_step=4 if stream_len >= 512 else 1,
        conv=min(256, stream_len),
        merge=min(256, rows),
        ffn=min(512, rows),
        ffn_cols=2048,
    )


def kernel(x_prompt, x_sample, state_hgrn, state_conv, meta_tokens, norm1_g, w_in, lb_logits, hgrn_norm_g,
           w_proj_h, dw_kernel, dw_bias, conv_ln_g, conv_ln_b, w_proj_c, w_out, norm2_g, w_ff1, w_ff2,
           final_norm_g):
    BP, TP, _ = x_prompt.shape
    BS, TS, _ = x_sample.shape

    lb = jnp.cumsum(jax.nn.softmax(lb_logits.astype(F32), axis=0), axis=0)[0].reshape(1, HW)
    g1 = norm1_g[0].reshape(1, D_MODEL)
    g2 = norm2_g[0].reshape(1, D_MODEL)
    gf = final_norm_g.reshape(1, D_MODEL)
    gn = hgrn_norm_g[0].reshape(1, DV)
    w_in_b = w_in[0].astype(BF16)
    wh_b = w_proj_h[0].astype(BF16)
    wc_b = w_proj_c[0].astype(BF16)
    wo_b = w_out[0].astype(BF16)
    w1_b = w_ff1[0].astype(BF16)
    w2_b = w_ff2[0].astype(BF16)
    n_cols = CW // LANES
    dw = jnp.broadcast_to(dw_kernel[0].reshape(CONV_K, n_cols, 1, LANES).transpose(1, 0, 2, 3),
                          (n_cols, CONV_K, SUBLANES, LANES))
    dbias = jnp.broadcast_to(dw_bias[0].reshape(n_cols, 1, LANES), (n_cols, SUBLANES, LANES))
    lg = conv_ln_g[0].reshape(1, CW)
    lbias = conv_ln_b[0].reshape(1, CW)

    def layer(x2d, B, T, s0, hist, shared):
        tl = _tiles(B * T, T)
        z = _in_proj(x2d, g1, lb, w_in_b, tl["in_proj"])
        o, s_new = _hgrn(z, gn, s0, B, T, tl["chunk"], tl["chunks_per_step"], shared)
        c, h_new = _conv(z, hist, dw, dbias, lg, lbias, B, T, tl["conv"], shared)
        x1 = _merge(x2d, o, c, z, wh_b, wc_b, wo_b, tl["merge"])
        return x1, s_new, h_new

    def ffn(x1, B, T):
        tl = _tiles(B * T, T)
        return _ffn(x1, g2, w1_b, w2_b, gf, tl["ffn"], tl["ffn_cols"])

    zero_s = jnp.zeros((1, HEADS, DK, DV), F32)
    zero_h = jnp.zeros((1, HIST, CW), F32)
    meta_pad = jnp.pad(meta_tokens.astype(F32), ((META_CHUNK - N_META, 0), (0, 0)))
    _, s_meta, h_meta = layer(meta_pad, 1, META_CHUNK, zero_s, zero_h, False)

    xp = x_prompt.reshape(BP * TP, D_MODEL)
    x1p, sp, hp = layer(xp, BP, TP, s_meta, h_meta, True)
    yp = ffn(x1p, BP, TP)

    xs = x_sample.reshape(BS * TS, D_MODEL)
    hist_s = jnp.pad(state_conv[0], ((0, 0), (HIST_OFF, 0), (0, 0)))
    x1s, ss, hs = layer(xs, BS, TS, state_hgrn[0], hist_s, False)
    ys = ffn(x1s, BS, TS)

    return (yp.reshape(BP, TP, D_MODEL), ys.reshape(BS, TS, D_MODEL),
            sp[None], hp[:, HIST_OFF:][None], ss[None], hs[:, HIST_OFF:][None])
```

```python
import functools

import jax
import jax.numpy as jnp
from jax import lax
from jax.experimental import pallas as pl
from jax.experimental.pallas import tpu as pltpu

F32 = jnp.float32
BF16 = jnp.bfloat16

D_MODEL = 2048
N_META = 16
HEADS = 8
DK = 128
DV = 128
HW = HEADS * DK
CW = D_MODEL // 2
CONV_K = 31
D_FF = 4 * D_MODEL
N_IN = 4 * HW + 2 * CW + 2 * D_MODEL
EPS = 1e-6

SUBLANES = 8
LANES = 128
HIST = 32
HIST_OFF = HIST - (CONV_K - 1)
SUB = 16
HEAD_LAG = 2
META_CHUNK = 128
CONV_ROWS = 64
TN_IN = 2048

VMEM_LIMIT = 56 * 1024 * 1024
VMEM_LIMIT_WIDE = 60 * 1024 * 1024


def _cparams(sem, vmem_limit=VMEM_LIMIT):
    return pltpu.CompilerParams(dimension_semantics=sem, vmem_limit_bytes=vmem_limit)


def _rms(x, g):
    return x * lax.rsqrt(jnp.mean(x * x, axis=-1, keepdims=True) + EPS) * g


def _in_proj_kernel(x_ref, g_ref, lb_ref, w_ref, z_ref, h_ref):
    j = pl.program_id(1)
    lo = slice(0, HW)
    hi = slice(HW, TN_IN)

    @pl.when(j == 0)
    def _():
        h_ref[...] = _rms(x_ref[...], g_ref[...]).astype(BF16)

    def proj(cols):
        return jnp.dot(h_ref[...], w_ref[:, cols], preferred_element_type=F32)

    @pl.when(j == 0)
    def _():
        z_ref[:, lo] = jax.nn.silu(proj(lo))
        lb = lb_ref[...]
        z_ref[:, hi] = jnp.log(lb + (1.0 - lb) * jax.nn.sigmoid(proj(hi)))

    @pl.when(j == 1)
    def _():
        z_ref[:, lo] = proj(lo)
        z_ref[:, hi] = jax.nn.silu(proj(hi))

    @pl.when(j == 2)
    def _():
        gate = proj(hi)
        z_ref[:, hi] = gate
        z_ref[:, lo] = proj(lo) * jax.nn.sigmoid(gate)

    @pl.when(j >= 3)
    def _():
        z_ref[...] = proj(slice(0, TN_IN))


def _in_proj(x, g, lb, w, tm):
    T = x.shape[0]
    return pl.pallas_call(
        _in_proj_kernel,
        grid=(T // tm, N_IN // TN_IN),
        in_specs=[
            pl.BlockSpec((tm, D_MODEL), lambda i, j: (i, 0)),
            pl.BlockSpec((1, D_MODEL), lambda i, j: (0, 0)),
            pl.BlockSpec((1, HW), lambda i, j: (0, 0)),
            pl.BlockSpec((D_MODEL, TN_IN), lambda i, j: (0, j)),
        ],
        out_specs=pl.BlockSpec((tm, TN_IN), lambda i, j: (i, j)),
        out_shape=jax.ShapeDtypeStruct((T, N_IN), F32),
        scratch_shapes=[pltpu.VMEM((tm, D_MODEL), BF16)],
        compiler_params=_cparams(("parallel", "arbitrary"), VMEM_LIMIT_WIDE),
        name="in_proj",
    )(x, g, lb, w)


def _hgrn_kernel(q_ref, lf_ref, v_ref, og_ref, gn_ref, s0_ref, o_ref, s_ref, st_ref, *, C, nsub):
    c = pl.program_id(1)
    n = C // SUB
    mid = SUB // 2 - 1

    @pl.when(c == 0)
    def _():
        for h in range(HEADS):
            st_ref[h] = s0_ref[0, h]

    row = lax.broadcasted_iota(jnp.int32, (C, C), 0)
    col = lax.broadcasted_iota(jnp.int32, (C, C), 1)
    tril = col <= row
    srow = lax.broadcasted_iota(jnp.int32, (SUBLANES, C), 0)
    scol = lax.broadcasted_iota(jnp.int32, (SUBLANES, C), 1)
    sel = jnp.concatenate([tril, scol <= srow * SUB + mid], axis=0).astype(BF16)

    def cumulative(ci):
        lf = lf_ref[ci * C:(ci + 1) * C, :]
        hi = lf.astype(BF16)
        r1 = lf - hi.astype(F32)
        md = r1.astype(BF16)
        lo = (r1 - md.astype(F32)).astype(BF16)
        return (jnp.dot(sel, hi, preferred_element_type=F32)
                + jnp.dot(sel, md, preferred_element_type=F32)
                + jnp.dot(sel, lo, preferred_element_type=F32))

    b_alls = [cumulative(ci) for ci in range(nsub)]
    gn = gn_ref[...]
    zeros_blk = jnp.zeros((SUB, DK), BF16)

    def rows(a, i):
        return a[i * SUB:(i + 1) * SUB, :]

    def bcast(a, i):
        return jnp.broadcast_to(a[i:i + 1, :], (SUB, DK))

    def first_half(ci, h):
        rs = slice(ci * C, (ci + 1) * C)
        hs = slice(h * DK, (h + 1) * DK)
        b = b_alls[ci][:C, hs]
        rmat = b_alls[ci][C:, hs]
        g = b[C - 1:C, :]
        k = 1.0 - jnp.exp(lf_ref[rs, hs])
        q = q_ref[rs, hs]

        r_full = jnp.concatenate([bcast(rmat, i) for i in range(n)], axis=0) if n > 1 else bcast(rmat, 0)
        e = q * jnp.exp(b - r_full)
        kt = k * jnp.exp(r_full - b)
        to_state = jnp.exp(rmat)
        to_end = jnp.exp(g - rmat)

        a_cols, b_cols, qs_rows, kg_rows = [], [], [], []
        for j in range(n):
            fj = jnp.exp(rmat - rmat[j:j + 1, :])
            blocks = [zeros_blk] * j + [rows(e, j).astype(BF16)]
            blocks += [(rows(e, i) * bcast(fj, i)).astype(BF16) for i in range(j + 1, n)]
            a_cols.append(jnp.concatenate(blocks, axis=0) if n > 1 else blocks[0])
            kb = rows(kt, j).astype(BF16)
            blocks = [zeros_blk] * j + [kb] + [zeros_blk] * (n - 1 - j)
            b_cols.append(jnp.concatenate(blocks, axis=0) if n > 1 else blocks[0])
            qs_rows.append((rows(e, j) * bcast(to_state, j)).astype(BF16))
            kg_rows.append(rows(kt, j) * bcast(to_end, j))
        a_cat = jnp.concatenate(a_cols, axis=1) if n > 1 else a_cols[0]
        b_cat = jnp.concatenate(b_cols, axis=1) if n > 1 else b_cols[0]
        qs = jnp.concatenate(qs_rows, axis=0) if n > 1 else qs_rows[0]
        kg = jnp.concatenate(kg_rows, axis=0) if n > 1 else kg_rows[0]

        scores = lax.dot_general(a_cat, b_cat, (((1,), (1,)), ((), ())), preferred_element_type=F32)
        st = st_ref[h]
        decay = jnp.broadcast_to(jnp.exp(g), (DV, DK)).T
        st_ref[h] = st * decay + jnp.dot(kg.T.astype(BF16), v_ref[rs, hs].astype(BF16),
                                         preferred_element_type=F32)
        return scores, qs, st.astype(BF16)

    def second_half(ci, h, scores, qs, st_old):
        rs = slice(ci * C, (ci + 1) * C)
        hs = slice(h * DK, (h + 1) * DK)
        p = jnp.where(tril, scores, 0.0).astype(BF16)
        vb = v_ref[rs, hs].astype(BF16)
        if C % LANES == 0:
            o = jnp.dot(jnp.concatenate([p, qs], axis=1), jnp.concatenate([vb, st_old], axis=0),
                        preferred_element_type=F32)
        else:
            o = jnp.dot(p, vb, preferred_element_type=F32) + jnp.dot(qs, st_old, preferred_element_type=F32)
        o_ref[rs, hs] = (_rms(o, gn) * og_ref[rs, hs]).astype(o_ref.dtype)

    items = [(ci, h) for ci in range(nsub) for h in range(HEADS)]
    pending = {}
    for step in range(len(items) + HEAD_LAG):
        if step < len(items):
            pending[step] = first_half(*items[step])
        if step >= HEAD_LAG:
            second_half(*items[step - HEAD_LAG], *pending.pop(step - HEAD_LAG))

    @pl.when(c == pl.num_programs(1) - 1)
    def _():
        for h in range(HEADS):
            s_ref[0, h] = st_ref[h]


def _hgrn(z, gn, s0, B, T, C, nsub, shared_state):
    rows_per_step = C * nsub
    nc = T // rows_per_step
    s_idx = (lambda b, c: (0, 0, 0, 0)) if shared_state else (lambda b, c: (b, 0, 0, 0))

    def zcol(k):
        return pl.BlockSpec((rows_per_step, HW), lambda b, c: (b * nc + c, k))

    return pl.pallas_call(
        functools.partial(_hgrn_kernel, C=C, nsub=nsub),
        grid=(B, nc),
        in_specs=[
            zcol(0), zcol(1), zcol(2), zcol(3),
            pl.BlockSpec((1, DV), lambda b, c: (0, 0)),
            pl.BlockSpec((1, HEADS, DK, DV), s_idx),
        ],
        out_specs=[
            pl.BlockSpec((rows_per_step, HW), lambda b, c: (b * nc + c, 0)),
            pl.BlockSpec((1, HEADS, DK, DV), lambda b, c: (b, 0, 0, 0)),
        ],
        out_shape=[
            jax.ShapeDtypeStruct((B * T, HW), BF16),
            jax.ShapeDtypeStruct((B, HEADS, DK, DV), F32),
        ],
        scratch_shapes=[pltpu.VMEM((HEADS, DK, DV), F32)],
        compiler_params=_cparams(("parallel", "arbitrary")),
        name="hgrn",
    )(z, z, z, z, gn, s0)


def _conv_kernel(u_ref, hist_ref, w_ref, bias_ref, lg_ref, lbias_ref, c_ref, nh_ref,
                 uc_ref, dw_ref, *, tm):
    t = pl.program_id(1)
    n_cols = CW // LANES
    lane_cols = [slice(l * LANES, (l + 1) * LANES) for l in range(n_cols)]

    @pl.when(t == 0)
    def _():
        for l, cols in enumerate(lane_cols):
            uc_ref[l, 0:HIST, :] = hist_ref[0, :, cols]

    for l, cols in enumerate(lane_cols):
        uc_ref[l, HIST:HIST + tm, :] = u_ref[:, cols]

    rows = min(CONV_ROWS, tm)
    groups = rows // SUBLANES

    def lane_column(l, carry):
        for batch in range(tm // rows):
            accs = [bias_ref[l]] * groups
            for j in range(CONV_K):
                wv = w_ref[l, j]
                for g in range(groups):
                    lo = batch * rows + g * SUBLANES + HIST_OFF + j
                    accs[g] = accs[g] + uc_ref[l, lo:lo + SUBLANES, :] * wv
            for g in range(groups):
                lo = batch * rows + g * SUBLANES
                dw_ref[l, lo:lo + SUBLANES, :] = accs[g]
        return carry

    lax.fori_loop(0, n_cols, lane_column, 0)

    total = dw_ref[0]
    for l in range(1, n_cols):
        total = total + dw_ref[l]
    mu = jnp.sum(total, axis=-1, keepdims=True) * (1.0 / CW)
    sq = jnp.square(dw_ref[0] - mu)
    for l in range(1, n_cols):
        sq = sq + jnp.square(dw_ref[l] - mu)
    inv = lax.rsqrt(jnp.sum(sq, axis=-1, keepdims=True) * (1.0 / CW) + EPS)
    for l, cols in enumerate(lane_cols):
        y = (dw_ref[l] - mu) * inv * lg_ref[:, cols] + lbias_ref[:, cols]
        c_ref[:, cols] = jax.nn.silu(y).astype(c_ref.dtype)

    tails = [uc_ref[l, tm:tm + HIST, :] for l in range(n_cols)]
    for l in range(n_cols):
        uc_ref[l, 0:HIST, :] = tails[l]

    @pl.when(t == pl.num_programs(1) - 1)
    def _():
        for l, cols in enumerate(lane_cols):
            nh_ref[0, :, cols] = tails[l]


def _conv(z, hist, w, bias, lg, lbias, B, T, tm, shared_hist):
    nt = T // tm
    n_cols = CW // LANES
    h_idx = (lambda b, t: (0, 0, 0)) if shared_hist else (lambda b, t: (b, 0, 0))
    vec = pl.BlockSpec((1, CW), lambda b, t: (0, 0))
    return pl.pallas_call(
        functools.partial(_conv_kernel, tm=tm),
        grid=(B, nt),
        in_specs=[
            pl.BlockSpec((tm, CW), lambda b, t: (b * nt + t, 4)),
            pl.BlockSpec((1, HIST, CW), h_idx),
            pl.BlockSpec((n_cols, CONV_K, SUBLANES, LANES), lambda b, t: (0, 0, 0, 0)),
            pl.BlockSpec((n_cols, SUBLANES, LANES), lambda b, t: (0, 0, 0)),
            vec, vec,
        ],
        out_specs=[
            pl.BlockSpec((tm, CW), lambda b, t: (b * nt + t, 0)),
            pl.BlockSpec((1, HIST, CW), lambda b, t: (b, 0, 0)),
        ],
        out_shape=[
            jax.ShapeDtypeStruct((B * T, CW), BF16),
            jax.ShapeDtypeStruct((B, HIST, CW), F32),
        ],
        scratch_shapes=[
            pltpu.VMEM((n_cols, HIST + tm, LANES), F32),
            pltpu.VMEM((n_cols, tm, LANES), F32),
        ],
        compiler_params=_cparams(("parallel", "arbitrary")),
        name="conv",
    )(z, hist, w, bias, lg, lbias)


def _merge_kernel(x_ref, o_ref, c_ref, gh_ref, gc_ref, wh_ref, wc_ref, wo_ref, y_ref):
    bh = jnp.dot(o_ref[...], wh_ref[...], preferred_element_type=F32)
    bc = jnp.dot(c_ref[...], wc_ref[...], preferred_element_type=F32)
    m = jax.nn.sigmoid(gh_ref[...]) * bh + jax.nn.sigmoid(gc_ref[...]) * bc
    y_ref[...] = x_ref[...] + jnp.dot(m.astype(BF16), wo_ref[...], preferred_element_type=F32)


def _merge(x, o, c, z, wh, wc, wo, tm):
    T = x.shape[0]
    const = lambda i: (0, 0)
    return pl.pallas_call(
        _merge_kernel,
        grid=(T // tm,),
        in_specs=[
            pl.BlockSpec((tm, D_MODEL), lambda i: (i, 0)),
            pl.BlockSpec((tm, HW), lambda i: (i, 0)),
            pl.BlockSpec((tm, CW), lambda i: (i, 0)),
            pl.BlockSpec((tm, D_MODEL), lambda i: (i, 3)),
            pl.BlockSpec((tm, D_MODEL), lambda i: (i, 4)),
            pl.BlockSpec((HW, D_MODEL), const, pipeline_mode=pl.Buffered(1)),
            pl.BlockSpec((CW, D_MODEL), const, pipeline_mode=pl.Buffered(1)),
            pl.BlockSpec((D_MODEL, D_MODEL), const, pipeline_mode=pl.Buffered(1)),
        ],
        out_specs=pl.BlockSpec((tm, D_MODEL), lambda i: (i, 0)),
        out_shape=jax.ShapeDtypeStruct((T, D_MODEL), F32),
        compiler_params=_cparams(("parallel",)),
        name="merge",
    )(x, o, c, z, z, wh, wc, wo)


def _ffn_kernel(x_ref, g2_ref, w1_ref, w2_ref, gf_ref, y_ref, h_ref):
    j = pl.program_id(1)

    def mlp(h):
        hf = jnp.dot(h, w1_ref[...], preferred_element_type=F32)
        a = jnp.square(jnp.maximum(hf, 0.0)).astype(BF16)
        return jnp.dot(a, w2_ref[...], preferred_element_type=F32)

    @pl.when(j == 0)
    def _():
        x = x_ref[...]
        h = _rms(x, g2_ref[...]).astype(BF16)
        h_ref[...] = h
        y_ref[...] = x + mlp(h)

    last = pl.num_programs(1) - 1

    @pl.when((j > 0) & (j < last))
    def _():
        y_ref[...] += mlp(h_ref[...])

    @pl.when(j == last)
    def _():
        y_ref[...] = _rms(y_ref[...] + mlp(h_ref[...]), gf_ref[...])


def _ffn(x, g2, w1, w2, gf, tm, tf):
    T = x.shape[0]
    vec = pl.BlockSpec((1, D_MODEL), lambda i, j: (0, 0))
    return pl.pallas_call(
        _ffn_kernel,
        grid=(T // tm, D_FF // tf),
        in_specs=[
            pl.BlockSpec((tm, D_MODEL), lambda i, j: (i, 0)),
            vec,
            pl.BlockSpec((D_MODEL, tf), lambda i, j: (0, j)),
            pl.BlockSpec((tf, D_MODEL), lambda i, j: (j, 0)),
            vec,
        ],
        out_specs=pl.BlockSpec((tm, D_MODEL), lambda i, j: (i, 0)),
        out_shape=jax.ShapeDtypeStruct((T, D_MODEL), F32),
        scratch_shapes=[pltpu.VMEM((tm, D_MODEL), BF16)],
        compiler_params=_cparams(("parallel", "arbitrary"), VMEM_LIMIT_WIDE),
        name="ffn",
    )(x, g2, w1, w2, gf)


def _tiles(rows, stream_len):
    return dict(
        in_proj=min(1024, rows),
        chunk=min(128, stream_len),
        chunks_per_step=4 if stream_len >= 512 else 1,
        conv=min(256, stream_len),
        merge=min(256, rows),
        ffn=min(512, rows),
        ffn_cols=2048,
    )


def kernel(x_prompt, x_sample, state_hgrn, state_conv, meta_tokens, norm1_g, w_in, lb_logits, hgrn_norm_g,
           w_proj_h, dw_kernel, dw_bias, conv_ln_g, conv_ln_b, w_proj_c, w_out, norm2_g, w_ff1, w_ff2,
           final_norm_g):
    BP, TP, _ = x_prompt.shape
    BS, TS, _ = x_sample.shape

    lb = jnp.cumsum(jax.nn.softmax(lb_logits.astype(F32), axis=0), axis=0)[0].reshape(1, HW)
    g1 = norm1_g[0].reshape(1, D_MODEL)
    g2 = norm2_g[0].reshape(1, D_MODEL)
    gf = final_norm_g.reshape(1, D_MODEL)
    gn = hgrn_norm_g[0].reshape(1, DV)
    w_in_b = w_in[0].astype(BF16)
    wh_b = w_proj_h[0].astype(BF16)
    wc_b = w_proj_c[0].astype(BF16)
    wo_b = w_out[0].astype(BF16)
    w1_b = w_ff1[0].astype(BF16)
    w2_b = w_ff2[0].astype(BF16)
    n_cols = CW // LANES
    dw = jnp.broadcast_to(dw_kernel[0].reshape(CONV_K, n_cols, 1, LANES).transpose(1, 0, 2, 3),
                          (n_cols, CONV_K, SUBLANES, LANES))
    dbias = jnp.broadcast_to(dw_bias[0].reshape(n_cols, 1, LANES), (n_cols, SUBLANES, LANES))
    lg = conv_ln_g[0].reshape(1, CW)
    lbias = conv_ln_b[0].reshape(1, CW)

    def layer(x2d, B, T, s0, hist, shared):
        tl = _tiles(B * T, T)
        z = _in_proj(x2d, g1, lb, w_in_b, tl["in_proj"])
        o, s_new = _hgrn(z, gn, s0, B, T, tl["chunk"], tl["chunks_per_step"], shared)
        c, h_new = _conv(z, hist, dw, dbias, lg, lbias, B, T, tl["conv"], shared)
        x1 = _merge(x2d, o, c, z, wh_b, wc_b, wo_b, tl["merge"])
        return x1, s_new, h_new

    def ffn(x1, B, T):
        tl = _tiles(B * T, T)
        return _ffn(x1, g2, w1_b, w2_b, gf, tl["ffn"], tl["ffn_cols"])

    zero_s = jnp.zeros((1, HEADS, DK, DV), F32)
    zero_h = jnp.zeros((1, HIST, CW), F32)
    meta_pad = jnp.pad(meta_tokens.astype(F32), ((META_CHUNK - N_META, 0), (0, 0)))
    _, s_meta, h_meta = layer(meta_pad, 1, META_CHUNK, zero_s, zero_h, False)

    xp = x_prompt.reshape(BP * TP, D_MODEL)
    x1p, sp, hp = layer(xp, BP, TP, s_meta, h_meta, True)
    yp = ffn(x1p, BP, TP)

    xs = x_sample.reshape(BS * TS, D_MODEL)
    hist_s = jnp.pad(state_conv[0], ((0, 0), (HIST_OFF, 0), (0, 0)))
    x1s, ss, hs = layer(xs, BS, TS, state_hgrn[0], hist_s, False)
    ys = ffn(x1s, BS, TS)

    return (yp.reshape(BP, TP, D_MODEL), ys.reshape(BS, TS, D_MODEL),
            sp[None], hp[:, HIST_OFF:][None], ss[None], hs[:, HIST_OFF:][None])
```
